```python
import math
import jax
import jax.numpy as jnp
from jax import lax
import numpy as np

D_MODEL = 1024
BATCH = 4
SEQ = 4096
DEPTH = 4
DEC_BATCH = 32
DEC_SEQ = 4
PAST_LEN = 8192
PAGE_SIZE = 128

N_AB = (DEPTH + 1) // 2
N_CD = DEPTH // 2
H_A = 4
DK_A = 128
DV_A = 128
CONV_A = 4
CHUNK_A = 64
QKV_A = H_A * (2 * DK_A + DV_A)
H_B = 8
HKV_B = 2
DH_B = 64
H_IDX = 4
D_IDX = 64
TOPK_B = 256
C_W = 512
CONV_C = 31
H_D = 8
DH_D = 64
D_FF = 2816
N_EXP = 8
TOP_E = 2
D_FF_E = 3584
Q_BLOCK = 128
EPS = 1e-6
AB_IN_SPLITS = (QKV_A, H_A * DV_A, H_A, H_A, H_B * DH_B, HKV_B * DH_B, HKV_B * DH_B, H_IDX * D_IDX, D_IDX, H_IDX)
CD_IN_SPLITS = (2 * C_W, H_D * DH_D, H_D * DH_D, H_D * DH_D)
AB_OUT = H_A * DV_A + H_B * DH_B
CD_OUT = C_W + H_D * DH_D

kernel_name = 'hybrid_gdn_dsa_conformer_stickbreak_step'


def rms_norm(x, g):
    xf = x.astype(jnp.float32)
    y = xf * lax.rsqrt(jnp.mean(xf * xf, axis=-1, keepdims=True) + EPS)
    return (y * g.astype(jnp.float32)).astype(x.dtype)


def layer_norm(x, g, b):
    xf = x.astype(jnp.float32)
    mu = jnp.mean(xf, axis=-1, keepdims=True)
    xc = xf - mu
    y = xc * lax.rsqrt(jnp.mean(xc * xc, axis=-1, keepdims=True) + EPS)
    return (y * g.astype(jnp.float32) + b.astype(jnp.float32)).astype(x.dtype)


def l2_norm(x):
    xf = x.astype(jnp.float32)
    return xf * lax.rsqrt(jnp.sum(xf * xf, axis=-1, keepdims=True) + EPS)


def modulate(x, g, shift, scale):
    return rms_norm(x, g) * (1.0 + scale[:, None, :]) + shift[:, None, :]


def split_cols(x, sizes):
    return jnp.split(x, np.cumsum(sizes)[:-1].tolist(), axis=-1)


def causal_dwconv(x, buf, w, b):
    xp = jnp.concatenate([buf.astype(x.dtype), x], axis=1)
    y = lax.conv_general_dilated(xp, w[:, None, :].astype(x.dtype), window_strides=(1,), padding='VALID',
                                 dimension_numbers=('NWC', 'WIO', 'NWC'), feature_group_count=x.shape[-1])
    return y + b.astype(x.dtype), xp[:, xp.shape[1] - (w.shape[0] - 1):]


def gather_pages(pool, page_table, li):
    rows = pool[page_table, :, li]
    return rows.reshape((rows.shape[0], rows.shape[1] * rows.shape[2]) + rows.shape[3:])


def over_query_blocks(fn, q_pos, *qs):
    t = q_pos.shape[0]
    if t <= Q_BLOCK or t % Q_BLOCK:
        return fn(q_pos, *qs)
    nb = t // Q_BLOCK
    blocks = [jnp.moveaxis(a.reshape((a.shape[0], nb, Q_BLOCK) + a.shape[2:]), 1, 0) for a in qs]
    out = lax.map(lambda args: fn(*args), (q_pos.reshape(nb, Q_BLOCK), *blocks))
    out = jnp.moveaxis(out, 0, 1)
    return out.reshape((out.shape[0], t) + out.shape[3:])


def gated_delta_rule(q, k, v, log_decay, beta, s0):
    bsz, t, h, dk = q.shape
    dv = v.shape[-1]
    c = math.gcd(CHUNK_A, t)
    n = t // c

    def chunks(a):
        a = a.astype(jnp.float32).reshape((bsz, n, c, h) + a.shape[3:])
        return jnp.moveaxis(a, (1, 3), (0, 2))

    q, k, v, g, beta = (chunks(a) for a in (q, k, v, log_decay, beta))
    q = q * dk ** -0.5
    gc = jnp.cumsum(g, axis=-1)
    incl = jnp.tril(jnp.ones((c, c), dtype=bool))
    strict = jnp.tril(jnp.ones((c, c), dtype=bool), -1)
    diff = gc[..., :, None] - gc[..., None, :]
    decay = jnp.where(incl, jnp.exp(jnp.where(incl, diff, 0.0)), 0.0)
    kb = k * beta[..., None]
    a_low = jnp.where(strict, jnp.einsum('nbhid,nbhjd->nbhij', kb, k) * decay, 0.0)
    rhs = jnp.concatenate([v * beta[..., None], kb * jnp.exp(gc)[..., None]], axis=-1)
    uw = lax.linalg.triangular_solve(jnp.eye(c, dtype=jnp.float32) + a_low, rhs, left_side=True, lower=True)
    u, wk = uw[..., :dv], uw[..., dv:]
    qk = jnp.where(incl, jnp.einsum('nbhid,nbhjd->nbhij', q, k) * decay, 0.0)
    qg = q * jnp.exp(gc)[..., None]
    kd = k * jnp.exp(gc[..., -1:] - gc)[..., None]
    g_last = jnp.exp(gc[..., -1])

    def step(s, xs):
        u_i, w_i, qk_i, qg_i, kd_i, gl_i = xs
        v_new = u_i - jnp.einsum('bhck,bhkv->bhcv', w_i, s)
        o_i = jnp.einsum('bhck,bhkv->bhcv', qg_i, s) + jnp.einsum('bhij,bhjv->bhiv', qk_i, v_new)
        s = s * gl_i[..., None, None] + jnp.einsum('bhck,bhcv->bhkv', kd_i, v_new)
        return s, o_i

    s_final, o = lax.scan(step, s0.astype(jnp.float32), (u, wk, qk, qg, kd, g_last))
    o = jnp.moveaxis(o, (0, 2), (1, 3)).reshape(bsz, t, h, dv)
    return o, s_final


def dsa_attend(q_pos, q, q_idx, w_idx, k, v, k_idx):
    bsz, tq = q.shape[:2]
    n_keys = k.shape[1]
    n_sel = min(TOPK_B, n_keys // 4)
    admissible = jnp.arange(n_keys)[None, :] <= q_pos[:, None]
    rel = jax.nn.relu(jnp.einsum('bqhd,bkd->bqhk', q_idx.astype(jnp.float32), k_idx.astype(jnp.float32)) * D_IDX ** -0.5)
    score = jnp.einsum('bqhk,bqh->bqk', rel, w_idx.astype(jnp.float32))
    score = jnp.where(admissible[None], score, -1e30)
    _, sel = lax.top_k(score, n_sel)
    valid = sel <= q_pos[None, :, None]
    take = jax.vmap(lambda rows, ids: rows[ids])
    k_sel = take(k, sel).astype(jnp.float32)
    v_sel = take(v, sel).astype(jnp.float32)
    qg = q.astype(jnp.float32).reshape(bsz, tq, HKV_B, H_B // HKV_B, DH_B)
    logits = jnp.einsum('bqgrd,bqngd->bqgrn', qg, k_sel) * DH_B ** -0.5
    logits = jnp.where(valid[:, :, None, None, :], logits, -1e30)
    p = jax.nn.softmax(logits, axis=-1)
    o = jnp.einsum('bqgrn,bqngd->bqgrd', p, v_sel)
    return o.reshape(bsz, tq, H_B, DH_B).astype(q.dtype)


def stick_breaking_attend(q_pos, q, k, v):
    n_keys = k.shape[1]
    z = jnp.einsum('bqhd,bkhd->bhqk', q.astype(jnp.float32), k.astype(jnp.float32)) * DH_D ** -0.5
    before = jnp.arange(n_keys)[None, :] < q_pos[:, None]
    log_keep = jnp.where(before, jax.nn.log_sigmoid(-z), 0.0)
    later = lax.cumsum(log_keep, axis=3, reverse=True) - log_keep
    wts = jnp.where(before, jnp.exp(jax.nn.log_sigmoid(z) + later), 0.0)
    return jnp.einsum('bhqk,bkhd->bqhd', wts, v.astype(jnp.float32)).astype(q.dtype)


def mixer_ab(h, w, li, conv_buf, s0, past, pos0):
    bsz, t, _ = h.shape
    qkv_a, z_a, a_a, b_a, q_b, k_b, v_b, q_i, k_i, w_i = split_cols(h @ w['ab_w_in'][li], AB_IN_SPLITS)
    if conv_buf is None:
        conv_buf = jnp.zeros((bsz, CONV_A - 1, QKV_A), h.dtype)
    if s0 is None:
        s0 = jnp.zeros((bsz, H_A, DK_A, DV_A), jnp.float32)
    qkv_c, conv_new = causal_dwconv(qkv_a, conv_buf, w['ab_conv_w'][li], w['ab_conv_b'][li])
    q_a, k_a, v_a = split_cols(jax.nn.silu(qkv_c), (H_A * DK_A, H_A * DK_A, H_A * DV_A))
    q_a = l2_norm(q_a.reshape(bsz, t, H_A, DK_A))
    k_a = l2_norm(k_a.reshape(bsz, t, H_A, DK_A))
    v_a = v_a.reshape(bsz, t, H_A, DV_A)
    log_decay = -jnp.exp(w['ab_a_log'][li].astype(jnp.float32)) * jax.nn.softplus(
        a_a.astype(jnp.float32) + w['ab_dt_bias'][li].astype(jnp.float32))
    beta = jax.nn.sigmoid(b_a.astype(jnp.float32))
    o_a, s_new = gated_delta_rule(q_a, k_a, v_a, log_decay, beta, s0)
    o_a = rms_norm(o_a, w['ab_onorm'][li]) * jax.nn.silu(z_a.reshape(bsz, t, H_A, DV_A).astype(jnp.float32))
    q_b = rms_norm(q_b.reshape(bsz, t, H_B, DH_B), w['ab_qnorm'][li])
    k_b = rms_norm(k_b.reshape(bsz, t, HKV_B, DH_B), w['ab_knorm'][li])
    v_b = v_b.reshape(bsz, t, HKV_B, DH_B)
    q_i = q_i.reshape(bsz, t, H_IDX, D_IDX)
    w_i = w_i * H_IDX ** -0.5
    if past is None:
        keys_k, keys_v, keys_i = k_b, v_b, k_i
    else:
        keys_k = jnp.concatenate([past[0].astype(k_b.dtype), k_b], axis=1)
        keys_v = jnp.concatenate([past[1].astype(v_b.dtype), v_b], axis=1)
        keys_i = jnp.concatenate([past[2].astype(k_i.dtype), k_i], axis=1)
    q_pos = pos0 + jnp.arange(t)
    o_b = over_query_blocks(lambda qp, qq, qi, wi: dsa_attend(qp, qq, qi, wi, keys_k, keys_v, keys_i),
                            q_pos, q_b, q_i, w_i)
    mixed = jnp.concatenate([o_a.reshape(bsz, t, H_A * DV_A).astype(h.dtype), o_b.reshape(bsz, t, H_B * DH_B)], axis=-1)
    return mixed @ w['ab_w_out'][li], (s_new, conv_new, k_b, v_b, k_i)


def mixer_cd(h, w, li, conv_buf, past, pos0):
    bsz, t, _ = h.shape
    glu_in, q_d, k_d, v_d = split_cols(h @ w['cd_w_in'][li], CD_IN_SPLITS)
    glu_a, glu_b = jnp.split(glu_in, 2, axis=-1)
    u = glu_a * jax.nn.sigmoid(glu_b)
    if conv_buf is None:
        conv_buf = jnp.zeros((bsz, CONV_C - 1, C_W), h.dtype)
    uc, conv_new = causal_dwconv(u, conv_buf, w['cd_conv_w'][li], w['cd_conv_b'][li])
    o_c = jax.nn.silu(layer_norm(uc, w['cd_ln_g'][li], w['cd_ln_b'][li]))
    q_d = rms_norm(q_d.reshape(bsz, t, H_D, DH_D), w['cd_qnorm'][li])
    k_d = rms_norm(k_d.reshape(bsz, t, H_D, DH_D), w['cd_knorm'][li])
    v_d = v_d.reshape(bsz, t, H_D, DH_D)
    if past is None:
        keys_k, keys_v = k_d, v_d
    else:
        keys_k = jnp.concatenate([past[0].astype(k_d.dtype), k_d], axis=1)
        keys_v = jnp.concatenate([past[1].astype(v_d.dtype), v_d], axis=1)
    q_pos = pos0 + jnp.arange(t)
    o_d = over_query_blocks(lambda qp, qq: stick_breaking_attend(qp, qq, keys_k, keys_v), q_pos, q_d)
    mixed = jnp.concatenate([o_c, o_d.reshape(bsz, t, H_D * DH_D)], axis=-1)
    return mixed @ w['cd_w_out'][li], (conv_new, k_d, v_d)


def swiglu(h, wg, wu, wd):
    return (jax.nn.silu(h @ wg) * (h @ wu)) @ wd


def moe_swiglu(h, w_router, wg, wu, wd):
    logits = (h @ w_router).astype(jnp.float32)
    top_val, top_idx = lax.top_k(logits, TOP_E)
    probs = jax.nn.softmax(top_val, axis=-1)
    gate = jnp.sum(jax.nn.one_hot(top_idx, N_EXP, dtype=jnp.float32) * probs[..., None], axis=-2)
    y = jnp.zeros(h.shape, jnp.float32)
    for e in range(N_EXP):
        y = y + gate[..., e:e + 1] * swiglu(h, wg[e], wu[e], wd[e]).astype(jnp.float32)
    return y.astype(h.dtype)


def run_trunk(x, c, w, past, pos0):
    new = {name: [] for name in ('a_s', 'a_conv', 'b_k', 'b_v', 'b_kidx', 'c_conv', 'd_k', 'd_v')}
    c_act = jax.nn.silu(c)
    for layer in range(DEPTH):
        li = layer // 2
        mod = c_act @ w['w_ada'][layer] + w['b_ada'][layer]
        shift1, scale1, gate1, shift2, scale2, gate2 = jnp.split(mod, 6, axis=-1)
        h = modulate(x, w['norm_mix'][layer], shift1, scale1)
        if layer % 2 == 0:
            if past is None:
                conv0, s0, kv_past = None, None, None
            else:
                conv0 = past['a_conv'][li]
                s0 = past['a_s'][li]
                kv_past = (gather_pages(past['b_k'], past['page_table'], li),
                           gather_pages(past['b_v'], past['page_table'], li),
                           gather_pages(past['b_kidx'], past['page_table'], li))
            out, (s_new, conv_new, k_new, v_new, ki_new) = mixer_ab(h, w, li, conv0, s0, kv_past, pos0)
            new['a_s'].append(s_new)
            new['a_conv'].append(conv_new)
            new['b_k'].append(k_new)
            new['b_v'].append(v_new)
            new['b_kidx'].append(ki_new)
        else:
            if past is None:
                conv0, kv_past = None, None
            else:
                conv0 = past['c_conv'][li]
                kv_past = (gather_pages(past['d_k'], past['page_table'], li),
                           gather_pages(past['d_v'], past['page_table'], li))
            out, (conv_new, k_new, v_new) = mixer_cd(h, w, li, conv0, kv_past, pos0)
            new['c_conv'].append(conv_new)
            new['d_k'].append(k_new)
            new['d_v'].append(v_new)
        x = x + gate1[:, None, :] * out
        h = modulate(x, w['norm_ffn'][layer], shift2, scale2)
        if layer % 2 == 0:
            f = swiglu(h, w['ff_w_gate'][li], w['ff_w_up'][li], w['ff_w_down'][li])
        else:
            f = moe_swiglu(h, w['moe_router'][li], w['moe_w_gate'][li], w['moe_w_up'][li], w['moe_w_down'][li])
        x = x + gate2[:, None, :] * f
    stacked = (jnp.stack(new['a_s'], 0), jnp.stack(new['a_conv'], 0),
               jnp.stack(new['b_k'], 2), jnp.stack(new['b_v'], 2), jnp.stack(new['b_kidx'], 2),
               jnp.stack(new['c_conv'], 0), jnp.stack(new['d_k'], 2), jnp.stack(new['d_v'], 2))
    return x, stacked


def setup_inputs(seed: int = 0) -> dict:
    key = jax.random.key(seed)
    ks = iter(jax.random.split(key, 48))

    def nrm(shape, scale=1.0):
        return jax.random.normal(next(ks), shape, jnp.float32) * scale

    def gain(shape):
        return 1.0 + nrm(shape, 0.05)

    n_pages = PAST_LEN // PAGE_SIZE
    n_used = DEC_BATCH * n_pages
    n_pool = n_used + n_used // 4
    page_table = jax.random.permutation(next(ks), n_pool)[:n_used].reshape(DEC_BATCH, n_pages).astype(jnp.int32)
    dt = jnp.exp(jax.random.uniform(next(ks), (N_AB, H_A), jnp.float32, math.log(1e-3), math.log(1e-1)))
    return {
        'x_prompt': nrm((BATCH, SEQ, D_MODEL)),
        'x_sample': nrm((DEC_BATCH, DEC_SEQ, D_MODEL)),
        'state_a_s': nrm((N_AB, DEC_BATCH, H_A, DK_A, DV_A), 0.1),
        'state_a_conv': nrm((N_AB, DEC_BATCH, CONV_A - 1, QKV_A)),
        'cache_b_k': nrm((n_pool, PAGE_SIZE, N_AB, HKV_B, DH_B)),
        'cache_b_v': nrm((n_pool, PAGE_SIZE, N_AB, HKV_B, DH_B)),
        'cache_b_kidx': nrm((n_pool, PAGE_SIZE, N_AB, D_IDX)),
        'state_c_conv': nrm((N_CD, DEC_BATCH, CONV_C - 1, C_W), 0.5),
        'cache_d_k': nrm((n_pool, PAGE_SIZE, N_CD, H_D, DH_D)),
        'cache_d_v': nrm((n_pool, PAGE_SIZE, N_CD, H_D, DH_D)),
        'page_table': page_table,
        'c_prompt': nrm((BATCH, D_MODEL)),
        'c_sample': nrm((DEC_BATCH, D_MODEL)),
        'w_ada': nrm((DEPTH, D_MODEL, 6 * D_MODEL), 0.5 * D_MODEL ** -0.5),
        'b_ada': nrm((DEPTH, 6 * D_MODEL), 0.02),
        'norm_mix': gain((DEPTH, D_MODEL)),
        'norm_ffn': gain((DEPTH, D_MODEL)),
        'ab_w_in': nrm((N_AB, D_MODEL, sum(AB_IN_SPLITS)), D_MODEL ** -0.5),
        'ab_conv_w': nrm((N_AB, CONV_A, QKV_A), CONV_A ** -0.5),
        'ab_conv_b': nrm((N_AB, QKV_A), 0.02),
        'ab_a_log': jnp.log(jax.random.uniform(next(ks), (N_AB, H_A), jnp.float32, 1.0, 16.0)),
        'ab_dt_bias': dt + jnp.log(-jnp.expm1(-dt)),
        'ab_onorm': gain((N_AB, DV_A)),
        'ab_qnorm': gain((N_AB, DH_B)),
        'ab_knorm': gain((N_AB, DH_B)),
        'ab_w_out': nrm((N_AB, AB_OUT, D_MODEL), AB_OUT ** -0.5),
        'cd_w_in': nrm((N_CD, D_MODEL, sum(CD_IN_SPLITS)), D_MODEL ** -0.5),
        'cd_conv_w': nrm((N_CD, CONV_C, C_W), CONV_C ** -0.5),
        'cd_conv_b': nrm((N_CD, C_W), 0.02),
        'cd_ln_g': gain((N_CD, C_W)),
        'cd_ln_b': nrm((N_CD, C_W), 0.02),
        'cd_qnorm': gain((N_CD, DH_D)),
        'cd_knorm': gain((N_CD, DH_D)),
        'cd_w_out': nrm((N_CD, CD_OUT, D_MODEL), CD_OUT ** -0.5),
        'ff_w_gate': nrm((N_AB, D_MODEL, D_FF), D_MODEL ** -0.5),
        'ff_w_up': nrm((N_AB, D_MODEL, D_FF), D_MODEL ** -0.5),
        'ff_w_down': nrm((N_AB, D_FF, D_MODEL), D_FF ** -0.5),
        'moe_router': nrm((N_CD, D_MODEL, N_EXP), D_MODEL ** -0.5),
        'moe_w_gate': nrm((N_CD, N_EXP, D_MODEL, D_FF_E), D_MODEL ** -0.5),
        'moe_w_up': nrm((N_CD, N_EXP, D_MODEL, D_FF_E), D_MODEL ** -0.5),
        'moe_w_down': nrm((N_CD, N_EXP, D_FF_E, D_MODEL), D_FF_E ** -0.5),
    }


def reference(x_prompt, x_sample, state_a_s, state_a_conv, cache_b_k, cache_b_v, cache_b_kidx, state_c_conv,
              cache_d_k, cache_d_v, page_table, c_prompt, c_sample, w_ada, b_ada, norm_mix, norm_ffn,
              ab_w_in, ab_conv_w, ab_conv_b, ab_a_log, ab_dt_bias, ab_onorm, ab_qnorm, ab_knorm, ab_w_out,
              cd_w_in, cd_conv_w, cd_conv_b, cd_ln_g, cd_ln_b, cd_qnorm, cd_knorm, cd_w_out,
              ff_w_gate, ff_w_up, ff_w_down, moe_router, moe_w_gate, moe_w_up, moe_w_down):
    w = {'w_ada': w_ada, 'b_ada': b_ada, 'norm_mix': norm_mix, 'norm_ffn': norm_ffn,
         'ab_w_in': ab_w_in, 'ab_conv_w': ab_conv_w, 'ab_conv_b': ab_conv_b, 'ab_a_log': ab_a_log,
         'ab_dt_bias': ab_dt_bias, 'ab_onorm': ab_onorm, 'ab_qnorm': ab_qnorm, 'ab_knorm': ab_knorm,
         'ab_w_out': ab_w_out, 'cd_w_in': cd_w_in, 'cd_conv_w': cd_conv_w, 'cd_conv_b': cd_conv_b,
         'cd_ln_g': cd_ln_g, 'cd_ln_b': cd_ln_b, 'cd_qnorm': cd_qnorm, 'cd_knorm': cd_knorm,
         'cd_w_out': cd_w_out, 'ff_w_gate': ff_w_gate, 'ff_w_up': ff_w_up, 'ff_w_down': ff_w_down,
         'moe_router': moe_router, 'moe_w_gate': moe_w_gate, 'moe_w_up': moe_w_up, 'moe_w_down': moe_w_down}
    past = {'a_s': state_a_s, 'a_conv': state_a_conv, 'b_k': cache_b_k, 'b_v': cache_b_v,
            'b_kidx': cache_b_kidx, 'c_conv': state_c_conv, 'd_k': cache_d_k, 'd_v': cache_d_v,
            'page_table': page_table}
    y_prompt, (pa_s, pa_conv, pb_k, pb_v, pb_kidx, pc_conv, pd_k, pd_v) = run_trunk(x_prompt, c_prompt, w, None, 0)
    y_sample, (sa_s, sa_conv, sb_k, sb_v, sb_kidx, sc_conv, sd_k, sd_v) = run_trunk(x_sample, c_sample, w, past, PAST_LEN)
    return (y_prompt, y_sample, pa_s, sa_s, pa_conv, sa_conv, pb_k, sb_k, pb_v, sb_v, pb_kidx, sb_kidx,
            pc_conv, sc_conv, pd_k, sd_k, pd_v, sd_v)
```

```python
import functools
import math

import jax
import jax.numpy as jnp
from jax import lax
from jax.experimental import pallas as pl
from jax.experimental.pallas import tpu as pltpu

F32 = jnp.float32
BF16 = jnp.bfloat16
I32 = jnp.int32
HI = lax.Precision.HIGHEST

EPS = 1e-6
H_A, DK_A, DV_A, CONV_A, CHUNK_A = 4, 128, 128, 4, 64
QKV_A = H_A * (2 * DK_A + DV_A)
H_B, HKV_B, DH_B, H_IDX, D_IDX, TOPK_B = 8, 2, 64, 4, 64, 256
C_W, CONV_C = 512, 31
H_D, DH_D = 8, 64
N_EXP, TOP_E = 8, 2
NEG = -1e30
PAGES_PER_STEP = 8

SUBLANE, LANE = 8, 128
VMEM_LIMIT = 56 * 1024 * 1024

AB_COLS = 3328
OFF_QKV, OFF_Z, OFF_QB, OFF_KV, OFF_QI, OFF_KI, OFF_SMALL = 0, 1536, 2048, 2560, 2816, 3072, 3200


def _cparams(*sem):
    return pltpu.CompilerParams(dimension_semantics=sem, vmem_limit_bytes=VMEM_LIMIT)


def _sigmoid(x):
    return 1.0 / (1.0 + jnp.exp(-x))


def _silu(x):
    return x * _sigmoid(x)


def _softplus(x):
    return jnp.maximum(x, 0.0) + jnp.log(1.0 + jnp.exp(-jnp.abs(x)))


def _dot(a, b, **kw):
    return jnp.dot(a, b, preferred_element_type=F32, **kw)


def _dot_nt(a, b, **kw):
    return lax.dot_general(a, b, (((1,), (1,)), ((), ())), preferred_element_type=F32, **kw)


def _dot_tn(a, b, **kw):
    return lax.dot_general(a, b, (((0,), (0,)), ((), ())), preferred_element_type=F32, **kw)


def _split2(a):
    hi = a.astype(BF16)
    lo = (a - hi.astype(F32)).astype(BF16)
    return hi, lo


def _dot3(a, b, precise=False):
    if precise:
        return _dot(a, b, precision=HI)
    ah, al = _split2(a)
    bh, bl = _split2(b)
    return _dot(ah, bh) + (_dot(ah, bl) + _dot(al, bh))


def _dot3_nt(a, b, precise=False):
    if precise:
        return _dot_nt(a, b, precision=HI)
    ah, al = _split2(a)
    bh, bl = _split2(b)
    return _dot_nt(ah, bh) + (_dot_nt(ah, bl) + _dot_nt(al, bh))


def _mm(a, b, precise):
    if precise:
        return _dot(a.astype(F32), b.astype(F32), precision=HI)
    return _dot(a.astype(BF16), b.astype(BF16))


def _mm_nt(a, b, precise):
    if precise:
        return _dot_nt(a.astype(F32), b.astype(F32), precision=HI)
    return _dot_nt(a.astype(BF16), b.astype(BF16))


def _mm_tn(a, b, precise):
    if precise:
        return _dot_tn(a.astype(F32), b.astype(F32), precision=HI)
    return _dot_tn(a.astype(BF16), b.astype(BF16))


def _act_dtype(precise):
    return F32 if precise else BF16


def _modulated_norm(x, g, shift, scale):
    ms = jnp.mean(x * x, axis=-1, keepdims=True)
    return (x * lax.rsqrt(ms + EPS) * g) * (1.0 + scale) + shift


def _param_spec(r, tm, d, ngrid):
    if r == 1:
        if ngrid == 3:
            return pl.BlockSpec((None, 1, d), lambda g, i, j: (g, 0, 0))
        return pl.BlockSpec((None, 1, d), lambda g, i: (g, 0, 0))
    if ngrid == 3:
        return pl.BlockSpec((None, tm, d), lambda g, i, j: (g, i, 0))
    return pl.BlockSpec((None, tm, d), lambda g, i: (g, i, 0))


def _ada_kernel(c_ref, w_ref, b_ref, o_ref):
    a = _silu(c_ref[...])
    o_ref[...] = _dot(a, w_ref[...], precision=HI) + b_ref[...]


def ada_ln(c, w_ada, b_ada, tn=1536):
    r, d = c.shape
    depth, _, n = w_ada.shape
    return pl.pallas_call(
        _ada_kernel,
        grid=(depth, n // tn),
        in_specs=[pl.BlockSpec((r, d), lambda l, j: (0, 0)),
                  pl.BlockSpec((None, d, tn), lambda l, j: (l, 0, j)),
                  pl.BlockSpec((None, 1, tn), lambda l, j: (l, 0, j))],
        out_specs=pl.BlockSpec((None, r, tn), lambda l, j: (l, 0, j)),
        out_shape=jax.ShapeDtypeStruct((depth, r, n), F32),
        compiler_params=_cparams("parallel", "parallel"),
        name="ada_ln",
    )(c, w_ada, b_ada.reshape(depth, 1, n))


def _mod_linear_kernel(x_ref, g_ref, sh_ref, sc_ref, w_ref, o_ref, h_ref, *, precise):
    @pl.when(pl.program_id(2) == 0)
    def _():
        h_ref[...] = _modulated_norm(x_ref[...], g_ref[...], sh_ref[...], sc_ref[...]).astype(h_ref.dtype)

    o_ref[...] = _mm(h_ref[...], w_ref[...], precise)


def mod_linear(x, g, shift, scale, w, li, *, tm, tn, precise=False):
    gsz, t, d = x.shape
    n = w.shape[2]
    pspec = _param_spec(shift.shape[1], tm, d, 3)
    return pl.pallas_call(
        functools.partial(_mod_linear_kernel, precise=precise),
        grid=(gsz, t // tm, n // tn),
        in_specs=[pl.BlockSpec((None, tm, d), lambda g_, i, j: (g_, i, 0)),
                  pl.BlockSpec((1, d), lambda g_, i, j: (0, 0)),
                  pspec, pspec,
                  pl.BlockSpec((None, d, tn), lambda g_, i, j: (li, 0, j))],
        out_specs=pl.BlockSpec((None, tm, tn), lambda g_, i, j: (g_, i, j)),
        out_shape=jax.ShapeDtypeStruct((gsz, t, n), F32),
        scratch_shapes=[pltpu.VMEM((tm, d), _act_dtype(precise))],
        compiler_params=_cparams("parallel", "parallel", "arbitrary"),
        name="mod_linear",
    )(x, g, shift, scale, w)


def _head_norm_kernel(x_ref, g_ref, bd_ref, o_ref):
    x = x_ref[...]
    ms = _dot(x * x, bd_ref[...], precision=HI)
    o_ref[...] = x * lax.rsqrt(ms + EPS) * g_ref[...]


def head_norm(x, col_block, width, g, dh, *, tm):
    gsz, t, _ = x.shape
    idx = jnp.arange(width) // dh
    bd = (idx[:, None] == idx[None, :]).astype(F32) / dh
    gt = jnp.tile(g.astype(F32), width // dh).reshape(1, width)
    return pl.pallas_call(
        _head_norm_kernel,
        grid=(gsz, t // tm),
        in_specs=[pl.BlockSpec((None, tm, width), lambda g_, i: (g_, i, col_block)),
                  pl.BlockSpec((1, width), lambda g_, i: (0, 0)),
                  pl.BlockSpec((width, width), lambda g_, i: (0, 0))],
        out_specs=pl.BlockSpec((None, tm, width), lambda g_, i: (g_, i, 0)),
        out_shape=jax.ShapeDtypeStruct((gsz, t, width), F32),
        compiler_params=_cparams("parallel", "parallel"),
        name="head_norm",
    )(x, gt, bd)


def _gdn_kernel(qkv_ref, z_ref, ab_ref, conv0_ref, s0_ref, cw_ref, cb_ref, alog_ref, dtb_ref, onorm_ref,
                o_ref, convo_ref, so_ref, win_ref, s_ref, *, chunk, valid_len, precise):
    c = pl.program_id(1)
    cc = chunk
    hc = H_A * cc
    tail_rows = SUBLANE

    @pl.when(c == 0)
    def _():
        win_ref[0:tail_rows, :] = conv0_ref[...]
        s_ref[...] = s0_ref[...]

    win_ref[tail_rows:tail_rows + cc, :] = qkv_ref[...]
    acc = jnp.broadcast_to(cb_ref[...], (cc, QKV_A))
    for j in range(CONV_A):
        acc = acc + cw_ref[j:j + 1, :] * win_ref[pl.ds(tail_rows - (CONV_A - 1) + j, cc), :]
    y = _silu(acc)
    tail = win_ref[pl.ds(valid_len, tail_rows), :]
    win_ref[0:tail_rows, :] = tail
    convo_ref[...] = tail

    def l2n(a):
        return a * lax.rsqrt(jnp.sum(a * a, axis=-1, keepdims=True) + EPS)

    def stack(off, width):
        return jnp.concatenate([y[:, off + h * width: off + (h + 1) * width] for h in range(H_A)], axis=0)

    q = l2n(stack(0, DK_A)) * (DK_A ** -0.5)
    k = l2n(stack(H_A * DK_A, DK_A))
    v = stack(2 * H_A * DK_A, DV_A)
    ab = ab_ref[...]
    a_raw = jnp.concatenate([ab[:, h:h + 1] for h in range(H_A)], axis=0)
    b_raw = jnp.concatenate([ab[:, H_A + h:H_A + h + 1] for h in range(H_A)], axis=0)
    alog = jnp.concatenate([jnp.broadcast_to(alog_ref[:, h:h + 1], (cc, 1)) for h in range(H_A)], axis=0)
    dtb = jnp.concatenate([jnp.broadcast_to(dtb_ref[:, h:h + 1], (cc, 1)) for h in range(H_A)], axis=0)
    g_col = -jnp.exp(alog) * _softplus(a_raw + dtb)
    beta = _sigmoid(b_raw)
    rowi = lax.broadcasted_iota(I32, (hc, hc), 0)
    coli = lax.broadcasted_iota(I32, (hc, hc), 1)
    sh = int(math.log2(cc))
    if valid_len < cc:
        rv = (lax.broadcasted_iota(I32, (hc, 1), 0) & (cc - 1)) < valid_len
        g_col = jnp.where(rv, g_col, 0.0)
        beta = jnp.where(rv, beta, 0.0)
    same = (rowi >> sh) == (coli >> sh)
    eye = rowi == coli
    incl = same & (rowi >= coli)
    strict = same & (rowi > coli)

    def to_row(col):
        return jnp.sum(jnp.where(eye, col, 0.0), axis=0, keepdims=True)

    gc_col = jnp.sum(jnp.where(incl, to_row(g_col), 0.0), axis=1, keepdims=True)
    gc_row = to_row(gc_col)
    decay = jnp.where(incl, jnp.exp(jnp.where(incl, gc_col - gc_row, 0.0)), 0.0)
    kb = k * beta
    a_low = jnp.where(strict, _dot3_nt(kb, k, precise) * decay, 0.0)
    eg = jnp.exp(gc_col)
    rhs = jnp.concatenate([v * beta, kb * eg], axis=1)
    tinv = jnp.where(eye, 1.0, 0.0) - a_low
    p = _dot3(a_low, a_low, precise)
    tinv = tinv + _dot3(tinv, p, precise)
    for _ in range(int(math.log2(cc)) - 2):
        p = _dot3(p, p, precise)
        tinv = tinv + _dot3(tinv, p, precise)
    uw = _dot3(tinv, rhs, precise)
    u, wk = uw[:, :DV_A], uw[:, DV_A:]
    qk = jnp.where(incl, _mm_nt(q, k, precise) * decay, 0.0)
    qg = q * eg
    gc_last = jnp.concatenate(
        [jnp.broadcast_to(gc_col[h * cc + cc - 1:h * cc + cc, :], (cc, 1)) for h in range(H_A)], axis=0)
    kd = k * jnp.exp(gc_last - gc_col)
    v_new = []
    o_state = []
    for h in range(H_A):
        sl = slice(h * cc, (h + 1) * cc)
        s_h = s_ref[h].astype(_act_dtype(precise))
        v_new.append(u[sl] - _mm(wk[sl], s_h, precise))
        o_state.append(_mm(qg[sl], s_h, precise))
    v_new = jnp.concatenate(v_new, axis=0)
    o = jnp.concatenate(o_state, axis=0) + _mm(qk, v_new, precise)
    for h in range(H_A):
        sl = slice(h * cc, (h + 1) * cc)
        gl = jnp.exp(gc_col[h * cc + cc - 1:h * cc + cc, :])
        s_ref[h] = s_ref[h] * gl + _mm_tn(kd[sl], v_new[sl], precise)
    so_ref[...] = s_ref[...]
    on = o * lax.rsqrt(jnp.mean(o * o, axis=-1, keepdims=True) + EPS) * onorm_ref[...]
    z = z_ref[...]
    for h in range(H_A):
        o_ref[:, h * DV_A:(h + 1) * DV_A] = on[h * cc:(h + 1) * cc] * _silu(z[:, h * DV_A:(h + 1) * DV_A])


def gdn(proj, conv0, s0, cw, cb, alog, dtb, onorm, *, chunk, valid_len, precise=False):
    b, t, _ = proj.shape
    nc = t // chunk
    kern = functools.partial(_gdn_kernel, chunk=chunk, valid_len=valid_len, precise=precise)
    return pl.pallas_call(
        kern,
        grid=(b, nc),
        in_specs=[pl.BlockSpec((None, chunk, QKV_A), lambda i, c: (i, c, 0)),
                  pl.BlockSpec((None, chunk, H_A * DV_A), lambda i, c: (i, c, OFF_Z // (H_A * DV_A))),
                  pl.BlockSpec((None, chunk, LANE), lambda i, c: (i, c, OFF_SMALL // LANE)),
                  pl.BlockSpec((None, SUBLANE, QKV_A), lambda i, c: (i, 0, 0)),
                  pl.BlockSpec((None, H_A, DK_A, DV_A), lambda i, c: (i, 0, 0, 0)),
                  pl.BlockSpec((CONV_A, QKV_A), lambda i, c: (0, 0)),
                  pl.BlockSpec((1, QKV_A), lambda i, c: (0, 0)),
                  pl.BlockSpec((1, H_A), lambda i, c: (0, 0)),
                  pl.BlockSpec((1, H_A), lambda i, c: (0, 0)),
                  pl.BlockSpec((1, DV_A), lambda i, c: (0, 0))],
        out_specs=[pl.BlockSpec((None, chunk, H_A * DV_A), lambda i, c: (i, c, 0)),
                   pl.BlockSpec((None, SUBLANE, QKV_A), lambda i, c: (i, 0, 0)),
                   pl.BlockSpec((None, H_A, DK_A, DV_A), lambda i, c: (i, 0, 0, 0))],
        out_shape=[jax.ShapeDtypeStruct((b, t, H_A * DV_A), F32),
                   jax.ShapeDtypeStruct((b, SUBLANE, QKV_A), F32),
                   jax.ShapeDtypeStruct((b, H_A, DK_A, DV_A), F32)],
        scratch_shapes=[pltpu.VMEM((chunk + SUBLANE, QKV_A), F32),
                        pltpu.VMEM((H_A, DK_A, DV_A), F32)],
        compiler_params=_cparams("parallel", "arbitrary"),
        name="gdn",
    )(proj, proj, proj, conv0, s0, cw, cb.reshape(1, -1), alog.reshape(1, -1), dtb.reshape(1, -1),
      onorm.reshape(1, -1))


def _order_key(score):
    score = jnp.where(score == 0.0, 0.0, score)
    bits = pltpu.bitcast(score, I32)
    return jnp.where(bits < 0, bits ^ jnp.int32(0x7FFFFFFF), bits)


def _kth_largest_key(key, n_sel):
    rows = key.shape[0]

    def count_ge(cand):
        return jnp.sum(jnp.where(key >= cand, 1.0, 0.0), axis=1, keepdims=True)

    int_min = jnp.int32(-2 ** 31)
    zero = jnp.zeros((rows, 1), I32)
    ans = jnp.where(count_ge(zero) >= n_sel, zero, jnp.full((rows, 1), int_min, I32))

    def body(i, ans):
        cand = ans | (jnp.int32(1) << (jnp.int32(30) - i))
        return jnp.where(count_ge(cand) >= n_sel, cand, ans)

    return lax.fori_loop(0, 31, body, ans)


def _select_bias(key, thr, adm, n_sel, ustrict_ref):
    rows, n = key.shape
    need = n_sel - jnp.sum(jnp.where(key > thr, 1.0, 0.0), axis=1, keepdims=True)
    base = jnp.zeros((rows, 1), F32)
    pieces = []
    for c0 in range(0, n, LANE):
        e = jnp.where(key[:, c0:c0 + LANE] == thr, 1.0, 0.0)
        pieces.append(_dot(e.astype(BF16), ustrict_ref[...]) + base)
        base = base + jnp.sum(e, axis=1, keepdims=True)
    rank = jnp.concatenate(pieces, axis=1)
    sel = jnp.where(key > thr, 1.0, jnp.where((key == thr) & (rank < need), 1.0, 0.0))
    return jnp.where(adm & (sel > 0.5), 0.0, NEG)


def _strict_lower_ones(n):
    r = jnp.arange(n)
    return (r[:, None] < r[None, :]).astype(BF16)


def _strict_upper_ones(n):
    r = jnp.arange(n)
    return (r[:, None] > r[None, :]).astype(BF16)


def _dsa_kernel(qi_ref, sm_ref, kit_ref, q_ref, kt_ref, v_ref, us_ref, o_ref, *, tq, n_keys, n_sel):
    i = pl.program_id(1)
    q_pos = i * tq + lax.broadcasted_iota(I32, (tq, 1), 0)
    col = lax.broadcasted_iota(I32, (tq, n_keys), 1)
    adm = col <= q_pos
    qi = qi_ref[...]
    sm = sm_ref[...]
    kit = kit_ref[...]
    score = jnp.zeros((tq, n_keys), F32)
    for h in range(H_IDX):
        rel = jnp.maximum(_dot(qi[:, h * D_IDX:(h + 1) * D_IDX], kit, precision=HI) * (D_IDX ** -0.5), 0.0)
        score = score + rel * (sm[:, 2 * H_A + h:2 * H_A + h + 1] * (H_IDX ** -0.5))
    score = jnp.where(adm, score, NEG)
    key = _order_key(score)
    thr = _kth_largest_key(key, n_sel)
    bias = _select_bias(key, thr, adm, n_sel, us_ref)
    q = q_ref[...]
    v = v_ref[...]
    grp = H_B // HKV_B
    for h in range(H_B):
        g = h // grp
        logits = _dot(q[:, h * DH_B:(h + 1) * DH_B], kt_ref[g * DH_B:(g + 1) * DH_B, :]) * (DH_B ** -0.5) + bias
        m = jnp.max(logits, axis=-1, keepdims=True)
        p = jnp.exp(logits - m)
        s = jnp.sum(p, axis=-1, keepdims=True)
        ov = _dot(p.astype(BF16), v)
        o_ref[:, h * DH_B:(h + 1) * DH_B] = ov[:, g * DH_B:(g + 1) * DH_B] / s


def dsa_prompt(proj, qn, kt, v, kit, *, tq):
    b, t, _ = proj.shape
    n_sel = min(TOPK_B, t // 4)
    kern = functools.partial(_dsa_kernel, tq=tq, n_keys=t, n_sel=n_sel)
    wq = H_IDX * D_IDX
    return pl.pallas_call(
        kern,
        grid=(b, t // tq),
        in_specs=[pl.BlockSpec((None, tq, wq), lambda i, j: (i, j, OFF_QI // wq)),
                  pl.BlockSpec((None, tq, LANE), lambda i, j: (i, j, OFF_SMALL // LANE)),
                  pl.BlockSpec((None, D_IDX, t), lambda i, j: (i, 0, 0)),
                  pl.BlockSpec((None, tq, H_B * DH_B), lambda i, j: (i, j, 0)),
                  pl.BlockSpec((None, HKV_B * DH_B, t), lambda i, j: (i, 0, 0)),
                  pl.BlockSpec((None, t, HKV_B * DH_B), lambda i, j: (i, 0, 0)),
                  pl.BlockSpec((LANE, LANE), lambda i, j: (0, 0))],
        out_specs=pl.BlockSpec((None, tq, H_B * DH_B), lambda i, j: (i, j, 0)),
        out_shape=jax.ShapeDtypeStruct((b, t, H_B * DH_B), F32),
        compiler_params=_cparams("parallel", "parallel"),
        name="dsa_prompt",
    )(proj, proj, kit, qn, kt, v, _strict_lower_ones(LANE))


def _dsa_dec_kernel(pt_ref, qi_ref, w_ref, q_ref, knew_ref, vnew_ref, kinew_ref, us_ref, *rest,
                    npp, n_steps, n_past, n_sel, t_new, li):
    kp = rest[0:npp]
    vp = rest[npp:2 * npp]
    kip = rest[2 * npp:3 * npp]
    o_ref, k_all, v_all, ki_all = rest[3 * npp:]
    s = pl.program_id(1)
    page = knew_ref.shape[0]
    for r in range(npp):
        off = pl.multiple_of((s * npp + r) * page, page)
        k_all[pl.ds(off, page), :] = kp[r][...]
        v_all[pl.ds(off, page), :] = vp[r][...]
        ki_all[pl.ds(off, page), :] = kip[r][...]

    @pl.when(s == n_steps - 1)
    def _():
        n_keys = n_past + page
        k_all[n_past:n_keys, :] = knew_ref[...]
        v_all[n_past:n_keys, :] = vnew_ref[...]
        ki_all[n_past:n_keys, :] = kinew_ref[...]
        rows = SUBLANE
        col = lax.broadcasted_iota(I32, (rows, n_keys), 1)
        q_pos = n_past + lax.broadcasted_iota(I32, (rows, 1), 0)
        adm = (col <= q_pos) & (col < n_past + t_new)
        ki = ki_all[...]
        w = w_ref[...]
        score = jnp.zeros((rows, n_keys), F32)
        for h in range(H_IDX):
            rel = jnp.maximum(_dot_nt(qi_ref[h], ki, precision=HI) * (D_IDX ** -0.5), 0.0)
            score = score + rel * (w[:, h:h + 1] * (H_IDX ** -0.5))
        score = jnp.where(adm, score, NEG)
        key = _order_key(score)
        thr = _kth_largest_key(key, n_sel)
        bias = _select_bias(key, thr, adm, n_sel, us_ref)
        grp = H_B // HKV_B
        bias_g = jnp.concatenate([bias] * grp, axis=0)
        kk = k_all[...]
        vv = v_all[...]
        for g in range(HKV_B):
            logits = _dot_nt(q_ref[g], kk, precision=HI) * (DH_B ** -0.5) + bias_g
            m = jnp.max(logits, axis=-1, keepdims=True)
            p = jnp.exp(logits - m)
            ssum = jnp.sum(p, axis=-1, keepdims=True)
            og = _dot(p, vv, precision=HI) / ssum
            for hh in range(grp):
                h = g * grp + hh
                o_ref[:, h * DH_B:(h + 1) * DH_B] = og[hh * rows:(hh + 1) * rows, g * DH_B:(g + 1) * DH_B]


def dsa_decode(page_table, qi_pad, w_idx, q_pad, k_new, v_new, ki_new, pool_k, pool_v, pool_ki, li, *, npp, t_new):
    b, n_pages = page_table.shape
    page = pool_k.shape[1]
    n_steps = n_pages // npp
    n_past = n_pages * page
    n_sel = min(TOPK_B, (n_past + t_new) // 4)
    rows = SUBLANE * (H_B // HKV_B)
    kern = functools.partial(_dsa_dec_kernel, npp=npp, n_steps=n_steps, n_past=n_past, n_sel=n_sel,
                             t_new=t_new, li=li)

    def page_spec(r, lane_block):
        return pl.BlockSpec((None, page, LANE), lambda i, s, pt: (pt[i, s * npp + r], 0, lane_block))

    fixed4 = lambda i, s, pt: (i, 0, 0, 0)
    fixed3 = lambda i, s, pt: (i, 0, 0)
    in_specs = [pl.BlockSpec((None, H_IDX, SUBLANE, LANE), fixed4),
                pl.BlockSpec((None, SUBLANE, H_IDX), fixed3),
                pl.BlockSpec((None, HKV_B, rows, LANE), fixed4),
                pl.BlockSpec((None, page, LANE), fixed3),
                pl.BlockSpec((None, page, LANE), fixed3),
                pl.BlockSpec((None, page, LANE), fixed3),
                pl.BlockSpec((LANE, LANE), lambda i, s, pt: (0, 0))]
    in_specs += [page_spec(r, li) for r in range(npp)]
    in_specs += [page_spec(r, li) for r in range(npp)]
    in_specs += [page_spec(r, 0) for r in range(npp)]
    grid_spec = pltpu.PrefetchScalarGridSpec(
        num_scalar_prefetch=1,
        grid=(b, n_steps),
        in_specs=in_specs,
        out_specs=pl.BlockSpec((None, SUBLANE, H_B * DH_B), fixed3),
        scratch_shapes=[pltpu.VMEM((n_past + page, LANE), F32),
                        pltpu.VMEM((n_past + page, LANE), F32),
                        pltpu.VMEM((n_past + page, LANE), F32)])
    return pl.pallas_call(
        kern,
        grid_spec=grid_spec,
        out_shape=jax.ShapeDtypeStruct((b, SUBLANE, H_B * DH_B), F32),
        compiler_params=_cparams("parallel", "arbitrary"),
        name="dsa_decode",
    )(page_table, qi_pad, w_idx, q_pad, k_new, v_new, ki_new, _strict_lower_ones(LANE),
      *([pool_k] * npp), *([pool_v] * npp), *([pool_ki] * npp))


def _cconv_kernel(glu_ref, conv0_ref, cw_ref, cb_ref, lng_ref, lnb_ref, o_ref, convo_ref, win_ref, *, tt, valid_len):
    halo = 4 * SUBLANE

    @pl.when(pl.program_id(1) == 0)
    def _():
        win_ref[0:halo, :] = conv0_ref[...]

    glu = glu_ref[...]
    win_ref[halo:halo + tt, :] = glu[:, :C_W] * _sigmoid(glu[:, C_W:])
    acc = jnp.broadcast_to(cb_ref[...], (tt, C_W))
    for j in range(CONV_C):
        acc = acc + cw_ref[j:j + 1, :] * win_ref[pl.ds(halo - (CONV_C - 1) + j, tt), :]
    tail = win_ref[pl.ds(valid_len, halo), :]
    win_ref[0:halo, :] = tail
    convo_ref[...] = tail
    mu = jnp.mean(acc, axis=-1, keepdims=True)
    xc = acc - mu
    yn = xc * lax.rsqrt(jnp.mean(xc * xc, axis=-1, keepdims=True) + EPS) * lng_ref[...] + lnb_ref[...]
    o_ref[...] = _silu(yn)


def conformer_conv(proj, conv0, cw, cb, lng, lnb, *, tt, valid_len):
    b, t, _ = proj.shape
    halo = 4 * SUBLANE
    kern = functools.partial(_cconv_kernel, tt=tt, valid_len=valid_len)
    cwp = jnp.pad(cw, ((0, halo - CONV_C), (0, 0)))
    return pl.pallas_call(
        kern,
        grid=(b, t // tt),
        in_specs=[pl.BlockSpec((None, tt, 2 * C_W), lambda i, j: (i, j, 0)),
                  pl.BlockSpec((None, halo, C_W), lambda i, j: (i, 0, 0)),
                  pl.BlockSpec((halo, C_W), lambda i, j: (0, 0)),
                  pl.BlockSpec((1, C_W), lambda i, j: (0, 0)),
                  pl.BlockSpec((1, C_W), lambda i, j: (0, 0)),
                  pl.BlockSpec((1, C_W), lambda i, j: (0, 0))],
        out_specs=[pl.BlockSpec((None, tt, C_W), lambda i, j: (i, j, 0)),
                   pl.BlockSpec((None, halo, C_W), lambda i, j: (i, 0, 0))],
        out_shape=[jax.ShapeDtypeStruct((b, t, C_W), F32),
                   jax.ShapeDtypeStruct((b, halo, C_W), F32)],
        scratch_shapes=[pltpu.VMEM((tt + halo, C_W), F32)],
        compiler_params=_cparams("parallel", "arbitrary"),
        name="conformer_conv",
    )(proj, conv0, cwp, cb.reshape(1, -1), lng.reshape(1, -1), lnb.reshape(1, -1))


def _sb_block(q, kj, vj, u_ref, carry, before, scale, precise=False):
    run, acc = carry
    z = _mm_nt(q, kj, precise) * scale
    lk = -_softplus(z)
    if before is not None:
        lk = jnp.where(before, lk, 0.0)
    if precise:
        later = _dot(lk, u_ref[...].astype(F32), precision=HI)
    else:
        hi, lo = _split2(lk)
        later = _dot(hi, u_ref[...]) + _dot(lo, u_ref[...])
    w = jnp.exp(z + lk + later + run)
    if before is not None:
        w = jnp.where(before, w, 0.0)
    acc = acc + _mm(w, vj, precise)
    run = run + later[:, 0:1] + lk[:, 0:1]
    return run, acc


def _sb_kernel(q_ref, k_ref, v_ref, u_ref, o_ref, *, tq):
    i = pl.program_id(2)
    q = q_ref[...]
    scale = DH_D ** -0.5
    rowi = lax.broadcasted_iota(I32, (tq, tq), 0)
    coli = lax.broadcasted_iota(I32, (tq, tq), 1)

    def blk(j, carry, before):
        off = pl.multiple_of(j * tq, tq)
        return _sb_block(q, k_ref[pl.ds(off, tq), :], v_ref[pl.ds(off, tq), :], u_ref, carry, before, scale)

    carry = (jnp.zeros((tq, 1), F32), jnp.zeros((tq, DH_D), F32))
    carry = blk(i, carry, coli < rowi)
    carry = lax.fori_loop(0, i, lambda jj, c: blk(i - 1 - jj, c, None), carry)
    o_ref[...] = carry[1]


def sb_prompt(q, k, v, *, tq):
    b, h, t, dh = q.shape
    kern = functools.partial(_sb_kernel, tq=tq)
    return pl.pallas_call(
        kern,
        grid=(b, h, t // tq),
        in_specs=[pl.BlockSpec((None, None, tq, dh), lambda i, j, l: (i, j, l, 0)),
                  pl.BlockSpec((None, None, t, dh), lambda i, j, l: (i, j, 0, 0)),
                  pl.BlockSpec((None, None, t, dh), lambda i, j, l: (i, j, 0, 0)),
                  pl.BlockSpec((tq, tq), lambda i, j, l: (0, 0))],
        out_specs=pl.BlockSpec((None, None, tq, dh), lambda i, j, l: (i, j, l, 0)),
        out_shape=jax.ShapeDtypeStruct((b, h, t, dh), F32),
        compiler_params=_cparams("parallel", "parallel", "parallel"),
        name="sb_prompt",
    )(q, k, v, _strict_upper_ones(tq))


def _sb_dec_kernel(pt_ref, qb_ref, knew_ref, vnew_ref, u_ref, *rest, npp, n_steps):
    kp = rest[0:npp]
    vp = rest[npp:2 * npp]
    o_ref, run_ref, acc_ref = rest[2 * npp:]
    s = pl.program_id(1)
    q = qb_ref[...]
    rows = q.shape[0]
    page = knew_ref.shape[0]
    scale = DH_D ** -0.5

    @pl.when(s == 0)
    def _():
        qidx = lax.broadcasted_iota(I32, (rows, page), 0) >> int(math.log2(H_D))
        coli = lax.broadcasted_iota(I32, (rows, page), 1)
        carry = (jnp.zeros((rows, 1), F32), jnp.zeros((rows, H_D * DH_D), F32))
        run, acc = _sb_block(q, knew_ref[...], vnew_ref[...], u_ref, carry, coli < qidx, scale, precise=True)
        run_ref[...] = jnp.broadcast_to(run, run_ref.shape)
        acc_ref[...] = acc

    carry = (run_ref[:, 0:1], acc_ref[...])
    for r in range(npp):
        carry = _sb_block(q, kp[r][...], vp[r][...], u_ref, carry, None, scale, precise=True)
    run_ref[...] = jnp.broadcast_to(carry[0], run_ref.shape)
    acc_ref[...] = carry[1]

    @pl.when(s == n_steps - 1)
    def _():
        width = acc_ref.shape[1]
        lane_head = lax.broadcasted_iota(I32, (H_D, width), 1) >> int(math.log2(DH_D))
        own = lane_head == lax.broadcasted_iota(I32, (H_D, width), 0)
        per_q = carry[1].reshape(rows // H_D, H_D, width)
        o_ref[...] = jnp.sum(jnp.where(own[None], per_q, 0.0), axis=1)


def sb_decode(page_table, qb, k_new, v_new, pool_k, pool_v, li, *, npp):
    b, n_pages = page_table.shape
    page = pool_k.shape[1]
    rows, width = qb.shape[1], qb.shape[2]
    n_steps = n_pages // npp
    kern = functools.partial(_sb_dec_kernel, npp=npp, n_steps=n_steps)

    def page_spec(r):
        return pl.BlockSpec((None, page, width), lambda i, s, pt: (pt[i, n_pages - 1 - (s * npp + r)], 0, li))

    fixed3 = lambda i, s, pt: (i, 0, 0)
    in_specs = [pl.BlockSpec((None, rows, width), fixed3),
                pl.BlockSpec((None, page, width), fixed3),
                pl.BlockSpec((None, page, width), fixed3),
                pl.BlockSpec((page, page), lambda i, s, pt: (0, 0))]
    in_specs += [page_spec(r) for r in range(npp)] * 2
    grid_spec = pltpu.PrefetchScalarGridSpec(
        num_scalar_prefetch=1,
        grid=(b, n_steps),
        in_specs=in_specs,
        out_specs=pl.BlockSpec((None, rows // H_D, width), fixed3),
        scratch_shapes=[pltpu.VMEM((rows, LANE), F32), pltpu.VMEM((rows, width), F32)])
    return pl.pallas_call(
        kern,
        grid_spec=grid_spec,
        out_shape=jax.ShapeDtypeStruct((b, rows // H_D, width), F32),
        compiler_params=_cparams("parallel", "arbitrary"),
        name="sb_decode",
    )(page_table, qb, k_new, v_new, _strict_upper_ones(page), *([pool_k] * npp), *([pool_v] * npp))


def _out_res_kernel(a_ref, b_ref, wa_ref, wb_ref, x_ref, gt_ref, o_ref, *, precise):
    y = _mm(a_ref[...], wa_ref[...], precise) + _mm(b_ref[...], wb_ref[...], precise)
    o_ref[...] = x_ref[...] + gt_ref[...] * y


def out_residual(a, bm, w, li, x, gate, *, tm, precise=False):
    gsz, t, d = x.shape
    ka, kb = a.shape[2], bm.shape[2]
    assert ka == kb
    return pl.pallas_call(
        functools.partial(_out_res_kernel, precise=precise),
        grid=(gsz, t // tm),
        in_specs=[pl.BlockSpec((None, tm, ka), lambda g_, i: (g_, i, 0)),
                  pl.BlockSpec((None, tm, kb), lambda g_, i: (g_, i, 0)),
                  pl.BlockSpec((None, ka, d), lambda g_, i: (li, 0, 0)),
                  pl.BlockSpec((None, kb, d), lambda g_, i: (li, 1, 0)),
                  pl.BlockSpec((None, tm, d), lambda g_, i: (g_, i, 0)),
                  _param_spec(gate.shape[1], tm, d, 2)],
        out_specs=pl.BlockSpec((None, tm, d), lambda g_, i: (g_, i, 0)),
        out_shape=jax.ShapeDtypeStruct((gsz, t, d), F32),
        compiler_params=_cparams("parallel", "parallel"),
        name="out_residual",
    )(a, bm, w, w, x, gate)


def _ffn_kernel(x_ref, g_ref, sh_ref, sc_ref, gt_ref, wg_ref, wu_ref, wd_ref, o_ref, h_ref, acc_ref, *, precise):
    k = pl.program_id(2)

    @pl.when(k == 0)
    def _():
        h_ref[...] = _modulated_norm(x_ref[...], g_ref[...], sh_ref[...], sc_ref[...]).astype(h_ref.dtype)
        acc_ref[...] = jnp.zeros_like(acc_ref)

    h = h_ref[...]
    a = _mm(h, wg_ref[...], precise)
    u = _mm(h, wu_ref[...], precise)
    acc_ref[...] += _mm(_silu(a) * u, wd_ref[...], precise)

    @pl.when(k == pl.num_programs(2) - 1)
    def _():
        o_ref[...] = x_ref[...] + gt_ref[...] * acc_ref[...]


def ffn_residual(x, g, shift, scale, gate, wg, wu, wd, li, *, tm, tf, precise=False):
    gsz, t, d = x.shape
    f = wg.shape[2]
    pspec = _param_spec(shift.shape[1], tm, d, 3)
    return pl.pallas_call(
        functools.partial(_ffn_kernel, precise=precise),
        grid=(gsz, t // tm, f // tf),
        in_specs=[pl.BlockSpec((None, tm, d), lambda g_, i, k: (g_, i, 0)),
                  pl.BlockSpec((1, d), lambda g_, i, k: (0, 0)),
                  pspec, pspec, pspec,
                  pl.BlockSpec((None, d, tf), lambda g_, i, k: (li, 0, k)),
                  pl.BlockSpec((None, d, tf), lambda g_, i, k: (li, 0, k)),
                  pl.BlockSpec((None, tf, d), lambda g_, i, k: (li, k, 0))],
        out_specs=pl.BlockSpec((None, tm, d), lambda g_, i, k: (g_, i, 0)),
        out_shape=jax.ShapeDtypeStruct((gsz, t, d), F32),
        scratch_shapes=[pltpu.VMEM((tm, d), _act_dtype(precise)), pltpu.VMEM((tm, d), F32)],
        compiler_params=_cparams("parallel", "parallel", "arbitrary"),
        name="ffn_residual",
    )(x, g, shift, scale, gate, wg, wu, wd)


def _router_kernel(x_ref, g_ref, sh_ref, sc_ref, wr_ref, h_ref, r_ref):
    h = _modulated_norm(x_ref[...], g_ref[...], sh_ref[...], sc_ref[...])
    h_ref[...] = h
    logits = _dot(h, wr_ref[...], precision=HI)
    lane = lax.broadcasted_iota(I32, logits.shape, 1)
    lanef = lane.astype(F32)
    logits = jnp.where(lane < N_EXP, logits, -jnp.inf)
    v1 = jnp.max(logits, axis=-1, keepdims=True)
    i1 = jnp.min(jnp.where(logits == v1, lanef, float(LANE)), axis=-1, keepdims=True)
    rest = jnp.where(lanef == i1, -jnp.inf, logits)
    v2 = jnp.max(rest, axis=-1, keepdims=True)
    i2 = jnp.min(jnp.where(rest == v2, lanef, float(LANE)), axis=-1, keepdims=True)
    e2 = jnp.exp(v2 - v1)
    p1 = 1.0 / (1.0 + e2)
    p2 = e2 / (1.0 + e2)
    r_ref[...] = jnp.where(lane == 0, i1, jnp.where(lane == 1, i2, jnp.where(lane == 2, p1, jnp.where(lane == 3, p2, 0.0))))


def moe_router(x, g, shift, scale, w_router, li, *, tm):
    gsz, t, d = x.shape
    wr = jnp.pad(w_router[li], ((0, 0), (0, LANE - N_EXP)))
    pspec = _param_spec(shift.shape[1], tm, d, 2)
    return pl.pallas_call(
        _router_kernel,
        grid=(gsz, t // tm),
        in_specs=[pl.BlockSpec((None, tm, d), lambda g_, i: (g_, i, 0)),
                  pl.BlockSpec((1, d), lambda g_, i: (0, 0)),
                  pspec, pspec,
                  pl.BlockSpec((d, LANE), lambda g_, i: (0, 0))],
        out_specs=[pl.BlockSpec((None, tm, d), lambda g_, i: (g_, i, 0)),
                   pl.BlockSpec((None, tm, LANE), lambda g_, i: (g_, i, 0))],
        out_shape=[jax.ShapeDtypeStruct((gsz, t, d), F32),
                   jax.ShapeDtypeStruct((gsz, t, LANE), F32)],
        compiler_params=_cparams("parallel", "parallel"),
        name="moe_router",
    )(x, g, shift, scale, wr)


def _row_copy(src_ref, dst_ref, src_row, dst_row, sem):
    return pltpu.make_async_copy(src_ref.at[pl.ds(src_row, 1), :], dst_ref.at[pl.ds(dst_row, 1), :], sem)


def _gather_kernel(idx_ref, src_ref, o_ref, sem, *, tg):
    base = pl.program_id(0) * tg

    def issue(r, carry):
        _row_copy(src_ref, o_ref, idx_ref[base + r], r, sem).start()
        return carry

    lax.fori_loop(0, tg, issue, 0)

    def drain(r, carry):
        _row_copy(src_ref, o_ref, 0, r, sem).wait()
        return carry

    lax.fori_loop(0, tg, drain, 0)


def gather_rows(src, idx, *, tg):
    p = idx.shape[0]
    d = src.shape[1]
    grid_spec = pltpu.PrefetchScalarGridSpec(
        num_scalar_prefetch=1,
        grid=(p // tg,),
        in_specs=[pl.BlockSpec(memory_space=pl.ANY)],
        out_specs=pl.BlockSpec((tg, d), lambda i, idx_: (i, 0)),
        scratch_shapes=[pltpu.SemaphoreType.DMA(())])
    return pl.pallas_call(
        functools.partial(_gather_kernel, tg=tg),
        grid_spec=grid_spec,
        out_shape=jax.ShapeDtypeStruct((p, d), F32),
        compiler_params=_cparams("arbitrary"),
        name="gather_rows",
    )(idx, src)


def _expert_kernel(te_ref, act_ref, xs_ref, gate_ref, wg_ref, wu_ref, wd_ref, o_ref, h_ref, acc_ref, *, precise):
    i = pl.program_id(0)
    k = pl.program_id(1)
    nk = pl.num_programs(1)
    active = act_ref[i] == 1

    @pl.when(active & (k == 0))
    def _():
        h_ref[...] = xs_ref[...].astype(h_ref.dtype)
        acc_ref[...] = jnp.zeros_like(acc_ref)

    @pl.when(active)
    def _():
        h = h_ref[...]
        a = _mm(h, wg_ref[...], precise)
        u = _mm(h, wu_ref[...], precise)
        acc_ref[...] += _mm(_silu(a) * u, wd_ref[...], precise)

    @pl.when(active & (k == nk - 1))
    def _():
        o_ref[...] = gate_ref[...] * acc_ref[...]

    @pl.when(jnp.logical_not(active) & (k == nk - 1))
    def _():
        o_ref[...] = jnp.zeros_like(o_ref)


def moe_experts(xs, row_gate, tile_expert, tile_active, wg, wu, wd, li, *, tm, tf, precise=False):
    p, d = xs.shape
    f = wg.shape[3]
    nk = f // tf

    def kk(i, k, act):
        return jnp.where(act[i] == 1, k, nk - 1)

    grid_spec = pltpu.PrefetchScalarGridSpec(
        num_scalar_prefetch=2,
        grid=(p // tm, nk),
        in_specs=[pl.BlockSpec((tm, d), lambda i, k, te, act: (i, 0)),
                  pl.BlockSpec((tm, 1), lambda i, k, te, act: (i, 0)),
                  pl.BlockSpec((None, None, d, tf), lambda i, k, te, act: (li, te[i], 0, kk(i, k, act))),
                  pl.BlockSpec((None, None, d, tf), lambda i, k, te, act: (li, te[i], 0, kk(i, k, act))),
                  pl.BlockSpec((None, None, tf, d), lambda i, k, te, act: (li, te[i], kk(i, k, act), 0))],
        out_specs=pl.BlockSpec((tm, d), lambda i, k, te, act: (i, 0)),
        scratch_shapes=[pltpu.VMEM((tm, d), _act_dtype(precise)), pltpu.VMEM((tm, d), F32)])
    return pl.pallas_call(
        functools.partial(_expert_kernel, precise=precise),
        grid_spec=grid_spec,
        out_shape=jax.ShapeDtypeStruct((p, d), F32),
        compiler_params=_cparams("arbitrary", "arbitrary"),
        name="moe_experts",
    )(tile_expert, tile_active, xs, row_gate, wg, wu, wd)


def _combine_kernel(p0_ref, p1_ref, ys_ref, x_ref, gt_ref, o_ref, a_ref, b_ref, sem, *, tc, t):
    base = pl.program_id(0) * t + pl.program_id(1) * tc

    def issue(r, carry):
        _row_copy(ys_ref, a_ref, p0_ref[base + r], r, sem.at[0]).start()
        _row_copy(ys_ref, b_ref, p1_ref[base + r], r, sem.at[1]).start()
        return carry

    lax.fori_loop(0, tc, issue, 0)

    def drain(r, carry):
        _row_copy(ys_ref, a_ref, 0, r, sem.at[0]).wait()
        _row_copy(ys_ref, b_ref, 0, r, sem.at[1]).wait()
        return carry

    lax.fori_loop(0, tc, drain, 0)
    o_ref[...] = x_ref[...] + gt_ref[...] * (a_ref[...] + b_ref[...])


def moe_combine(ys, pos0, pos1, x, gate, *, tc):
    gsz, t, d = x.shape
    if gate.shape[1] == 1:
        gspec = pl.BlockSpec((None, 1, d), lambda g_, i, a, b: (g_, 0, 0))
    else:
        gspec = pl.BlockSpec((None, tc, d), lambda g_, i, a, b: (g_, i, 0))
    grid_spec = pltpu.PrefetchScalarGridSpec(
        num_scalar_prefetch=2,
        grid=(gsz, t // tc),
        in_specs=[pl.BlockSpec(memory_space=pl.ANY),
                  pl.BlockSpec((None, tc, d), lambda g_, i, a, b: (g_, i, 0)),
                  gspec],
        out_specs=pl.BlockSpec((None, tc, d), lambda g_, i, a, b: (g_, i, 0)),
        scratch_shapes=[pltpu.VMEM((tc, d), F32), pltpu.VMEM((tc, d), F32), pltpu.SemaphoreType.DMA((2,))])
    return pl.pallas_call(
        functools.partial(_combine_kernel, tc=tc, t=t),
        grid_spec=grid_spec,
        out_shape=jax.ShapeDtypeStruct((gsz, t, d), F32),
        compiler_params=_cparams("arbitrary", "arbitrary"),
        name="moe_combine",
    )(pos0, pos1, ys, x, gate)


def _routing_tables(idx, probs, tm):
    m = idx.shape[0]
    e_flat = idx.reshape(-1)
    onehot = (e_flat[:, None] == jnp.arange(N_EXP, dtype=I32)[None, :]).astype(I32)
    rank = jnp.take_along_axis(jnp.cumsum(onehot, axis=0) - onehot, e_flat[:, None], axis=1)[:, 0]
    counts = jnp.sum(onehot, axis=0)
    padded = ((counts + tm - 1) // tm) * tm
    starts = jnp.cumsum(padded) - padded
    pos = starts[e_flat] + rank
    n_rows = ((2 * m + N_EXP * (tm - 1)) // tm) * tm
    n_tiles = n_rows // tm
    row_token = jnp.zeros((n_rows,), I32).at[pos].set(jnp.arange(2 * m, dtype=I32) // 2)
    row_gate = jnp.zeros((n_rows,), F32).at[pos].set(probs.reshape(-1))
    tile_start = jnp.arange(n_tiles, dtype=I32) * tm
    ends = jnp.cumsum(padded)
    tile_e = jnp.sum((tile_start[:, None] >= ends[None, :]).astype(I32), axis=1)
    total = ends[-1]
    active = (tile_start < total).astype(I32)
    last_e = jnp.max(jnp.where(counts > 0, jnp.arange(N_EXP, dtype=I32), 0))
    tile_e = jnp.where(active == 1, jnp.minimum(tile_e, N_EXP - 1), last_e).astype(I32)
    return row_token, row_gate.reshape(-1, 1), tile_e, active, pos.reshape(m, 2)


def moe_residual(x, g, shift, scale, gate, w_router, wg, wu, wd, li, *, tm_tok, tm, tf, precise=False):
    gsz, t, d = x.shape
    h, r = moe_router(x, g, shift, scale, w_router, li, tm=tm_tok)
    r = r.reshape(gsz * t, LANE)
    idx = r[:, 0:2].astype(I32)
    probs = r[:, 2:4]
    row_token, row_gate, tile_e, active, pos = _routing_tables(idx, probs, tm)
    xs = gather_rows(h.reshape(gsz * t, d), row_token, tg=min(tm, 256))
    ys = moe_experts(xs, row_gate, tile_e, active, wg, wu, wd, li, tm=tm, tf=tf, precise=precise)
    return moe_combine(ys, pos[:, 0], pos[:, 1], x, gate, tc=min(tm_tok, 256))


def _pad_ab_weight(w):
    s = [QKV_A, H_A * DV_A, H_A, H_A, H_B * DH_B, HKV_B * DH_B, HKV_B * DH_B, H_IDX * D_IDX, D_IDX, H_IDX]
    o = [0]
    for n in s:
        o.append(o[-1] + n)
    qkv, z, a, b, qb, kb, vb, qi, ki, wi = (w[:, :, o[j]:o[j + 1]] for j in range(10))
    zeros = lambda n: jnp.zeros(w.shape[:2] + (n,), w.dtype)
    return jnp.concatenate([qkv, z, qb, kb, vb, qi, ki, zeros(LANE - D_IDX), a, b, wi, zeros(LANE - 2 * H_A - H_IDX)], axis=-1)


def _tok_tile(t, want):
    return want if t % want == 0 else t


def _trunk(x, mods, w, past, ab_w_pad, tokens_per_seq):
    depth = w['w_ada'].shape[0]
    gsz, t, d = x.shape
    nseq = gsz * t // tokens_per_seq
    ts = tokens_per_seq
    decode = past is not None
    new = {name: [] for name in ('a_s', 'a_conv', 'b_k', 'b_v', 'b_kidx', 'c_conv', 'd_k', 'd_v')}
    tm = _tok_tile(t, 512)

    def seq(a):
        return a.reshape(nseq, ts, a.shape[-1])

    def tok(a):
        return a.reshape(gsz, t, a.shape[-1])

    for layer in range(depth):
        li = layer // 2
        shift1, scale1, gate1, shift2, scale2, gate2 = mods[layer]
        g_mix = w['norm_mix'][layer].reshape(1, d)
        g_ffn = w['norm_ffn'][layer].reshape(1, d)
        if layer % 2 == 0:
            proj = mod_linear(x, g_mix, shift1, scale1, ab_w_pad, li, tm=tm, tn=AB_COLS // 2, precise=decode)
            qn = head_norm(proj, OFF_QB // 512, 512, w['ab_qnorm'][li], DH_B, tm=tm)
            kn = head_norm(proj, OFF_KV // 128, 128, w['ab_knorm'][li], DH_B, tm=tm)
            projs = seq(proj)
            v_b = projs[:, :, OFF_KV + 128:OFF_KV + 256]
            k_i = projs[:, :, OFF_KI:OFF_KI + D_IDX]
            kns = seq(kn)
            if not decode:
                conv0 = jnp.zeros((nseq, SUBLANE, QKV_A), F32)
                s0 = jnp.zeros((nseq, H_A, DK_A, DV_A), F32)
                o_a, conv_t, s_new = gdn(projs, conv0, s0, w['ab_conv_w'][li], w['ab_conv_b'][li], w['ab_a_log'][li],
                                         w['ab_dt_bias'][li], w['ab_onorm'][li], chunk=CHUNK_A, valid_len=CHUNK_A)
                o_b = dsa_prompt(projs, seq(qn).astype(BF16), jnp.swapaxes(kns, 1, 2).astype(BF16),
                                 v_b.astype(BF16), jnp.swapaxes(k_i, 1, 2), tq=128)
            else:
                cpad = SUBLANE
                conv0 = jnp.pad(past['a_conv'][li], ((0, 0), (SUBLANE - (CONV_A - 1), 0), (0, 0)))
                proj_p = jnp.pad(projs, ((0, 0), (0, cpad - ts), (0, 0)))
                o_a, conv_t, s_new = gdn(proj_p, conv0, past['a_s'][li], w['ab_conv_w'][li], w['ab_conv_b'][li],
                                         w['ab_a_log'][li], w['ab_dt_bias'][li], w['ab_onorm'][li],
                                         chunk=cpad, valid_len=ts, precise=True)
                o_a = o_a[:, :ts]
                o_b = _dsa_decode_wrap(projs, seq(qn), kns, v_b, k_i, past, li, ts)
            new['a_s'].append(s_new)
            new['a_conv'].append(conv_t[:, SUBLANE - (CONV_A - 1):])
            new['b_k'].append(kns.reshape(nseq, ts, HKV_B, DH_B))
            new['b_v'].append(v_b.reshape(nseq, ts, HKV_B, DH_B))
            new['b_kidx'].append(k_i)
            x = out_residual(tok(o_a), tok(o_b), w['ab_w_out'], li, x, gate1, tm=tm, precise=decode)
            x = ffn_residual(x, g_ffn, shift2, scale2, gate2, w['ff_w_gate'], w['ff_w_up'], w['ff_w_down'], li,
                             tm=_tok_tile(t, 1024), tf=256, precise=decode)
        else:
            proj = mod_linear(x, g_mix, shift1, scale1, w['cd_w_in'], li, tm=tm, tn=1280, precise=decode)
            qn = head_norm(proj, 2, 512, w['cd_qnorm'][li], DH_D, tm=tm)
            kn = head_norm(proj, 3, 512, w['cd_knorm'][li], DH_D, tm=tm)
            projs = seq(proj)
            v_d = projs[:, :, 4 * C_W:5 * C_W]
            kns, qns = seq(kn), seq(qn)
            halo = 4 * SUBLANE
            if not decode:
                conv0 = jnp.zeros((nseq, halo, C_W), F32)
                o_c, conv_t = conformer_conv(projs, conv0, w['cd_conv_w'][li], w['cd_conv_b'][li], w['cd_ln_g'][li],
                                             w['cd_ln_b'][li], tt=512, valid_len=512)
                heads = lambda a: a.reshape(nseq, ts, H_D, DH_D).transpose(0, 2, 1, 3).astype(BF16)
                o_d = sb_prompt(heads(qns), heads(kns), heads(v_d), tq=256)
                o_d = o_d.transpose(0, 2, 1, 3).reshape(nseq, ts, H_D * DH_D)
            else:
                conv0 = jnp.pad(past['c_conv'][li], ((0, 0), (halo - (CONV_C - 1), 0), (0, 0)))
                proj_p = jnp.pad(projs, ((0, 0), (0, SUBLANE - ts), (0, 0)))
                o_c, conv_t = conformer_conv(proj_p, conv0, w['cd_conv_w'][li], w['cd_conv_b'][li], w['cd_ln_g'][li],
                                             w['cd_ln_b'][li], tt=SUBLANE, valid_len=ts)
                o_c = o_c[:, :ts]
                o_d = _sb_decode_wrap(qns, kns, v_d, past, li, ts)
            new['c_conv'].append(conv_t[:, halo - (CONV_C - 1):])
            new['d_k'].append(kns.reshape(nseq, ts, H_D, DH_D))
            new['d_v'].append(v_d.reshape(nseq, ts, H_D, DH_D))
            x = out_residual(tok(o_c), tok(o_d), w['cd_w_out'], li, x, gate1, tm=tm, precise=decode)
            x = moe_residual(x, g_ffn, shift2, scale2, gate2, w['moe_router'], w['moe_w_gate'], w['moe_w_up'],
                             w['moe_w_down'], li, tm_tok=tm, tm=tm, tf=512, precise=decode)
    stacked = (jnp.stack(new['a_s'], 0), jnp.stack(new['a_conv'], 0),
               jnp.stack(new['b_k'], 2), jnp.stack(new['b_v'], 2), jnp.stack(new['b_kidx'], 2),
               jnp.stack(new['c_conv'], 0), jnp.stack(new['d_k'], 2), jnp.stack(new['d_v'], 2))
    return x, stacked


def _dsa_decode_wrap(projs, qns, kns, v_b, k_i, past, li, ts):
    nseq = projs.shape[0]
    page = past['b_k'].shape[1]
    n_layers = past['b_k'].shape[2]
    grp = H_B // HKV_B
    pad_rows = lambda a, n: jnp.pad(a, ((0, 0), (0, n - a.shape[1]), (0, 0)))
    qi = projs[:, :, OFF_QI:OFF_QI + H_IDX * D_IDX].reshape(nseq, ts, H_IDX, D_IDX).transpose(0, 2, 1, 3)
    qi = jnp.pad(qi, ((0, 0), (0, 0), (0, SUBLANE - ts), (li * D_IDX, LANE - (li + 1) * D_IDX)))
    w_idx = pad_rows(projs[:, :, OFF_SMALL + 2 * H_A:OFF_SMALL + 2 * H_A + H_IDX], SUBLANE)
    q = qns.reshape(nseq, ts, HKV_B, grp, DH_B).transpose(0, 2, 3, 1, 4)
    q = jnp.pad(q, ((0, 0), (0, 0), (0, 0), (0, SUBLANE - ts), (0, 0)))
    q = q.reshape(nseq, HKV_B, grp * SUBLANE, DH_B)
    q = jnp.stack([jnp.pad(q[:, g], ((0, 0), (0, 0), (g * DH_B, LANE - (g + 1) * DH_B))) for g in range(HKV_B)], 1)
    ki_new = jnp.pad(k_i, ((0, 0), (0, page - ts), (li * D_IDX, LANE - (li + 1) * D_IDX)))
    o = dsa_decode(past['page_table'], qi, w_idx, q, pad_rows(kns, page), pad_rows(v_b, page), ki_new,
                   past['b_k'].reshape(-1, page, n_layers * HKV_B * DH_B),
                   past['b_v'].reshape(-1, page, n_layers * HKV_B * DH_B),
                   past['b_kidx'].reshape(-1, page, n_layers * D_IDX), li,
                   npp=math.gcd(PAGES_PER_STEP, past['page_table'].shape[1]), t_new=ts)
    return o[:, :ts]


def _sb_decode_wrap(qns, kns, v_d, past, li, ts):
    nseq = qns.shape[0]
    page = past['d_k'].shape[1]
    n_layers = past['d_k'].shape[2]
    width = H_D * DH_D
    q = qns.reshape(nseq, ts, H_D, DH_D)
    eye = jnp.eye(H_D, dtype=F32)
    qb = (q[:, :, :, None, :] * eye[None, None, :, :, None]).reshape(nseq, ts * H_D, width)
    pad_rows = lambda a: jnp.pad(a, ((0, 0), (0, page - ts), (0, 0)))
    o = sb_decode(past['page_table'], qb, pad_rows(kns), pad_rows(v_d),
                  past['d_k'].reshape(-1, page, n_layers * width), past['d_v'].reshape(-1, page, n_layers * width),
                  li, npp=math.gcd(PAGES_PER_STEP, past['page_table'].shape[1]))
    return o


def kernel(x_prompt, x_sample, state_a_s, state_a_conv, cache_b_k, cache_b_v, cache_b_kidx, state_c_conv, cache_d_k, cache_d_v, page_table, c_prompt, c_sample, w_ada, b_ada, norm_mix, norm_ffn, ab_w_in, ab_conv_w, ab_conv_b, ab_a_log, ab_dt_bias, ab_onorm, ab_qnorm, ab_knorm, ab_w_out, cd_w_in, cd_conv_w, cd_conv_b, cd_ln_g, cd_ln_b, cd_qnorm, cd_knorm, cd_w_out, ff_w_gate, ff_w_up, ff_w_down, moe_router, moe_w_gate, moe_w_up, moe_w_down):
    w = {'w_ada': w_ada, 'b_ada': b_ada, 'norm_mix': norm_mix, 'norm_ffn': norm_ffn,
         'ab_conv_w': ab_conv_w, 'ab_conv_b': ab_conv_b, 'ab_a_log': ab_a_log,
         'ab_dt_bias': ab_dt_bias, 'ab_onorm': ab_onorm, 'ab_qnorm': ab_qnorm, 'ab_knorm': ab_knorm,
         'ab_w_out': ab_w_out, 'cd_w_in': cd_w_in, 'cd_conv_w': cd_conv_w, 'cd_conv_b': cd_conv_b,
         'cd_ln_g': cd_ln_g, 'cd_ln_b': cd_ln_b, 'cd_qnorm': cd_qnorm, 'cd_knorm': cd_knorm,
         'cd_w_out': cd_w_out, 'ff_w_gate': ff_w_gate, 'ff_w_up': ff_w_up, 'ff_w_down': ff_w_down,
         'moe_router': moe_router, 'moe_w_gate': moe_w_gate, 'moe_w_up': moe_w_up, 'moe_w_down': moe_w_down}
    past = {'a_s': state_a_s, 'a_conv': state_a_conv, 'b_k': cache_b_k, 'b_v': cache_b_v,
            'b_kidx': cache_b_kidx, 'c_conv': state_c_conv, 'd_k': cache_d_k, 'd_v': cache_d_v,
            'page_table': page_table}
    bp, tp, d = x_prompt.shape
    bs, tsq, _ = x_sample.shape
    depth = w_ada.shape[0]
    rows = bp + bs
    rpad = -rows % SUBLANE
    c_all = jnp.pad(jnp.concatenate([c_prompt, c_sample], axis=0), ((0, rpad), (0, 0)))
    mod = ada_ln(c_all, w_ada, b_ada)
    mod = mod.reshape(depth, rows + rpad, 6, d)
    mods_p = [[mod[l, :bp, j].reshape(bp, 1, d) for j in range(6)] for l in range(depth)]
    mods_s = [[jnp.repeat(mod[l, bp:bp + bs, j], tsq, axis=0).reshape(1, bs * tsq, d) for j in range(6)]
              for l in range(depth)]
    ab_w_pad = _pad_ab_weight(ab_w_in)
    y_p, (pa_s, pa_conv, pb_k, pb_v, pb_kidx, pc_conv, pd_k, pd_v) = _trunk(x_prompt, mods_p, w, None, ab_w_pad, tp)
    y_s, (sa_s, sa_conv, sb_k, sb_v, sb_kidx, sc_conv, sd_k, sd_v) = _trunk(
        x_sample.reshape(1, bs * tsq, d), mods_s, w, past, ab_w_pad, tsq)
    return (y_p, y_s.reshape(bs, tsq, d), pa_s, sa_s, pa_conv, sa_conv, pb_k, sb_k, pb_v, sb_v, pb_kidx, sb_kidx,
            pc_conv, sc_conv, pd_k, sd_k, pd_v, sd_v)
```

```python
import functools
import math

import jax
import jax.numpy as jnp
from jax import lax
from jax.experimental import pallas as pl
from jax.experimental.pallas import tpu as pltpu

F32 = jnp.float32
BF16 = jnp.bfloat16
I32 = jnp.int32
HI = lax.Precision.HIGHEST

EPS = 1e-6
H_A, DK_A, DV_A, CONV_A, CHUNK_A = 4, 128, 128, 4, 64
QKV_A = H_A * (2 * DK_A + DV_A)
H_B, HKV_B, DH_B, H_IDX, D_IDX, TOPK_B = 8, 2, 64, 4, 64, 256
C_W, CONV_C = 512, 31
H_D, DH_D = 8, 64
N_EXP, TOP_E = 8, 2
NEG = -1e30
PAGES_PER_STEP = 8

SUBLANE, LANE = 8, 128
VMEM_LIMIT = 56 * 1024 * 1024

AB_COLS = 3328
OFF_QKV, OFF_Z, OFF_QB, OFF_KV, OFF_QI, OFF_KI, OFF_SMALL = 0, 1536, 2048, 2560, 2816, 3072, 3200


def _cparams(*sem):
    return pltpu.CompilerParams(dimension_semantics=sem, vmem_limit_bytes=VMEM_LIMIT)


def _sigmoid(x):
    return 1.0 / (1.0 + jnp.exp(-x))


def _silu(x):
    return x * _sigmoid(x)


def _softplus(x):
    return jnp.maximum(x, 0.0) + jnp.log(1.0 + jnp.exp(-jnp.abs(x)))


def _dot(a, b, **kw):
    return jnp.dot(a, b, preferred_element_type=F32, **kw)


def _dot_nt(a, b, **kw):
    return lax.dot_general(a, b, (((1,), (1,)), ((), ())), preferred_element_type=F32, **kw)


def _dot_tn(a, b, **kw):
    return lax.dot_general(a, b, (((0,), (0,)), ((), ())), preferred_element_type=F32, **kw)


def _split2(a):
    hi = a.astype(BF16)
    lo = (a - hi.astype(F32)).astype(BF16)
    return hi, lo


def _dot3(a, b, precise=False):
    if precise:
        return _dot(a, b, precision=HI)
    ah, al = _split2(a)
    bh, bl = _split2(b)
    return _dot(ah, bh) + (_dot(ah, bl) + _dot(al, bh))


def _dot3_nt(a, b, precise=False):
    if precise:
        return _dot_nt(a, b, precision=HI)
    ah, al = _split2(a)
    bh, bl = _split2(b)
    return _dot_nt(ah, bh) + (_dot_nt(ah, bl) + _dot_nt(al, bh))


def _mm(a, b, precise):
    if precise:
        return _dot(a.astype(F32), b.astype(F32), precision=HI)
    return _dot(a.astype(BF16), b.astype(BF16))


def _mm_nt(a, b, precise):
    if precise:
        return _dot_nt(a.astype(F32), b.astype(F32), precision=HI)
    return _dot_nt(a.astype(BF16), b.astype(BF16))


def _mm_tn(a, b, precise):
    if precise:
        return _dot_tn(a.astype(F32), b.astype(F32), precision=HI)
    return _dot_tn(a.astype(BF16), b.astype(BF16))


def _act_dtype(precise):
    return F32 if precise else BF16


def _modulated_norm(x, g, shift, scale):
    ms = jnp.mean(x * x, axis=-1, keepdims=True)
    return (x * lax.rsqrt(ms + EPS) * g) * (1.0 + scale) + shift


def _param_spec(r, tm, d, ngrid):
    if r == 1:
        if ngrid == 3:
            return pl.BlockSpec((None, 1, d), lambda g, i, j: (g, 0, 0))
        return pl.BlockSpec((None, 1, d), lambda g, i: (g, 0, 0))
    if ngrid == 3:
        return pl.BlockSpec((None, tm, d), lambda g, i, j: (g, i, 0))
    return pl.BlockSpec((None, tm, d), lambda g, i: (g, i, 0))


def _ada_kernel(c_ref, w_ref, b_ref, o_ref):
    a = _silu(c_ref[...])
    o_ref[...] = _dot(a, w_ref[...], precision=HI) + b_ref[...]


def ada_ln(c, w_ada, b_ada, tn=1536):
    r, d = c.shape
    depth, _, n = w_ada.shape
    return pl.pallas_call(
        _ada_kernel,
        grid=(depth, n // tn),
        in_specs=[pl.BlockSpec((r, d), lambda l, j: (0, 0)),
                  pl.BlockSpec((None, d, tn), lambda l, j: (l, 0, j)),
                  pl.BlockSpec((None, 1, tn), lambda l, j: (l, 0, j))],
        out_specs=pl.BlockSpec((None, r, tn), lambda l, j: (l, 0, j)),
        out_shape=jax.ShapeDtypeStruct((depth, r, n), F32),
        compiler_params=_cparams("parallel", "parallel"),
        name="ada_ln",
    )(c, w_ada, b_ada.reshape(depth, 1, n))


def _mod_linear_kernel(x_ref, g_ref, sh_ref, sc_ref, w_ref, o_ref, h_ref, *, precise):
    @pl.when(pl.program_id(2) == 0)
    def _():
        h_ref[...] = _modulated_norm(x_ref[...], g_ref[...], sh_ref[...], sc_ref[...]).astype(h_ref.dtype)

    o_ref[...] = _mm(h_ref[...], w_ref[...], precise)


def mod_linear(x, g, shift, scale, w, li, *, tm, tn, precise=False):
    gsz, t, d = x.shape
    n = w.shape[2]
    pspec = _param_spec(shift.shape[1], tm, d, 3)
    return pl.pallas_call(
        functools.partial(_mod_linear_kernel, precise=precise),
        grid=(gsz, t // tm, n // tn),
        in_specs=[pl.BlockSpec((None, tm, d), lambda g_, i, j: (g_, i, 0)),
                  pl.BlockSpec((1, d), lambda g_, i, j: (0, 0)),
                  pspec, pspec,
                  pl.BlockSpec((None, d, tn), lambda g_, i, j: (li, 0, j))],
        out_specs=pl.BlockSpec((None, tm, tn), lambda g_, i, j: (g_, i, j)),
        out_shape=jax.ShapeDtypeStruct((gsz, t, n), F32),
        scratch_shapes=[pltpu.VMEM((tm, d), _act_dtype(precise))],
        compiler_params=_cparams("parallel", "parallel", "arbitrary"),
        name="mod_linear",
    )(x, g, shift, scale, w)


def _head_norm_kernel(x_ref, g_ref, bd_ref, o_ref):
    x = x_ref[...]
    ms = _dot(x * x, bd_ref[...], precision=HI)
    o_ref[...] = x * lax.rsqrt(ms + EPS) * g_ref[...]


def head_norm(x, col_block, width, g, dh, *, tm):
    gsz, t, _ = x.shape
    idx = jnp.arange(width) // dh
    bd = (idx[:, None] == idx[None, :]).astype(F32) / dh
    gt = jnp.tile(g.astype(F32), width // dh).reshape(1, width)
    return pl.pallas_call(
        _head_norm_kernel,
        grid=(gsz, t // tm),
        in_specs=[pl.BlockSpec((None, tm, width), lambda g_, i: (g_, i, col_block)),
                  pl.BlockSpec((1, width), lambda g_, i: (0, 0)),
                  pl.BlockSpec((width, width), lambda g_, i: (0, 0))],
        out_specs=pl.BlockSpec((None, tm, width), lambda g_, i: (g_, i, 0)),
        out_shape=jax.ShapeDtypeStruct((gsz, t, width), F32),
        compiler_params=_cparams("parallel", "parallel"),
        name="head_norm",
    )(x, gt, bd)


def _gdn_kernel(qkv_ref, z_ref, ab_ref, conv0_ref, s0_ref, cw_ref, cb_ref, alog_ref, dtb_ref, onorm_ref,
                o_ref, convo_ref, so_ref, win_ref, s_ref, *, chunk, valid_len, precise):
    c = pl.program_id(1)
    cc = chunk
    hc = H_A * cc
    tail_rows = SUBLANE

    @pl.when(c == 0)
    def _():
        win_ref[0:tail_rows, :] = conv0_ref[...]
        s_ref[...] = s0_ref[...]

    win_ref[tail_rows:tail_rows + cc, :] = qkv_ref[...]
    acc = jnp.broadcast_to(cb_ref[...], (cc, QKV_A))
    for j in range(CONV_A):
        acc = acc + cw_ref[j:j + 1, :] * win_ref[pl.ds(tail_rows - (CONV_A - 1) + j, cc), :]
    y = _silu(acc)
    tail = win_ref[pl.ds(valid_len, tail_rows), :]
    win_ref[0:tail_rows, :] = tail
    convo_ref[...] = tail

    def l2n(a):
        return a * lax.rsqrt(jnp.sum(a * a, axis=-1, keepdims=True) + EPS)

    def stack(off, width):
        return jnp.concatenate([y[:, off + h * width: off + (h + 1) * width] for h in range(H_A)], axis=0)

    q = l2n(stack(0, DK_A)) * (DK_A ** -0.5)
    k = l2n(stack(H_A * DK_A, DK_A))
    v = stack(2 * H_A * DK_A, DV_A)
    ab = ab_ref[...]
    a_raw = jnp.concatenate([ab[:, h:h + 1] for h in range(H_A)], axis=0)
    b_raw = jnp.concatenate([ab[:, H_A + h:H_A + h + 1] for h in range(H_A)], axis=0)
    alog = jnp.concatenate([jnp.broadcast_to(alog_ref[:, h:h + 1], (cc, 1)) for h in range(H_A)], axis=0)
    dtb = jnp.concatenate([jnp.broadcast_to(dtb_ref[:, h:h + 1], (cc, 1)) for h in range(H_A)], axis=0)
    g_col = -jnp.exp(alog) * _softplus(a_raw + dtb)
    beta = _sigmoid(b_raw)
    rowi = lax.broadcasted_iota(I32, (hc, hc), 0)
    coli = lax.broadcasted_iota(I32, (hc, hc), 1)
    sh = int(math.log2(cc))
    if valid_len < cc:
        rv = (lax.broadcasted_iota(I32, (hc, 1), 0) & (cc - 1)) < valid_len
        g_col = jnp.where(rv, g_col, 0.0)
        beta = jnp.where(rv, beta, 0.0)
    same = (rowi >> sh) == (coli >> sh)
    eye = rowi == coli
    incl = same & (rowi >= coli)
    strict = same & (rowi > coli)

    def to_row(col):
        return jnp.sum(jnp.where(eye, col, 0.0), axis=0, keepdims=True)

    gc_col = jnp.sum(jnp.where(incl, to_row(g_col), 0.0), axis=1, keepdims=True)
    gc_row = to_row(gc_col)
    decay = jnp.where(incl, jnp.exp(jnp.where(incl, gc_col - gc_row, 0.0)), 0.0)
    kb = k * beta
    a_low = jnp.where(strict, _dot3_nt(kb, k, precise) * decay, 0.0)
    eg = jnp.exp(gc_col)
    rhs = jnp.concatenate([v * beta, kb * eg], axis=1)
    tinv = jnp.where(eye, 1.0, 0.0) - a_low
    p = _dot3(a_low, a_low, precise)
    tinv = tinv + _dot3(tinv, p, precise)
    for _ in range(int(math.log2(cc)) - 2):
        p = _dot3(p, p, precise)
        tinv = tinv + _dot3(tinv, p, precise)
    uw = _dot3(tinv, rhs, precise)
    u, wk = uw[:, :DV_A], uw[:, DV_A:]
    qk = jnp.where(incl, _mm_nt(q, k, precise) * decay, 0.0)
    qg = q * eg
    gc_last = jnp.concatenate(
        [jnp.broadcast_to(gc_col[h * cc + cc - 1:h * cc + cc, :], (cc, 1)) for h in range(H_A)], axis=0)
    kd = k * jnp.exp(gc_last - gc_col)
    v_new = []
    o_state = []
    for h in range(H_A):
        sl = slice(h * cc, (h + 1) * cc)
        s_h = s_ref[h].astype(_act_dtype(precise))
        v_new.append(u[sl] - _mm(wk[sl], s_h, precise))
        o_state.append(_mm(qg[sl], s_h, precise))
    v_new = jnp.concatenate(v_new, axis=0)
    o = jnp.concatenate(o_state, axis=0) + _mm(qk, v_new, precise)
    for h in range(H_A):
        sl = slice(h * cc, (h + 1) * cc)
        gl = jnp.exp(gc_col[h * cc + cc - 1:h * cc + cc, :])
        s_ref[h] = s_ref[h] * gl + _mm_tn(kd[sl], v_new[sl], precise)
    so_ref[...] = s_ref[...]
    on = o * lax.rsqrt(jnp.mean(o * o, axis=-1, keepdims=True) + EPS) * onorm_ref[...]
    z = z_ref[...]
    for h in range(H_A):
        o_ref[:, h * DV_A:(h + 1) * DV_A] = on[h * cc:(h + 1) * cc] * _silu(z[:, h * DV_A:(h + 1) * DV_A])


def gdn(proj, conv0, s0, cw, cb, alog, dtb, onorm, *, chunk, valid_len, precise=False):
    b, t, _ = proj.shape
    nc = t // chunk
    kern = functools.partial(_gdn_kernel, chunk=chunk, valid_len=valid_len, precise=precise)
    return pl.pallas_call(
        kern,
        grid=(b, nc),
        in_specs=[pl.BlockSpec((None, chunk, QKV_A), lambda i, c: (i, c, 0)),
                  pl.BlockSpec((None, chunk, H_A * DV_A), lambda i, c: (i, c, OFF_Z // (H_A * DV_A))),
                  pl.BlockSpec((None, chunk, LANE), lambda i, c: (i, c, OFF_SMALL // LANE)),
                  pl.BlockSpec((None, SUBLANE, QKV_A), lambda i, c: (i, 0, 0)),
                  pl.BlockSpec((None, H_A, DK_A, DV_A), lambda i, c: (i, 0, 0, 0)),
                  pl.BlockSpec((CONV_A, QKV_A), lambda i, c: (0, 0)),
                  pl.BlockSpec((1, QKV_A), lambda i, c: (0, 0)),
                  pl.BlockSpec((1, H_A), lambda i, c: (0, 0)),
                  pl.BlockSpec((1, H_A), lambda i, c: (0, 0)),
                  pl.BlockSpec((1, DV_A), lambda i, c: (0, 0))],
        out_specs=[pl.BlockSpec((None, chunk, H_A * DV_A), lambda i, c: (i, c, 0)),
                   pl.BlockSpec((None, SUBLANE, QKV_A), lambda i, c: (i, 0, 0)),
                   pl.BlockSpec((None, H_A, DK_A, DV_A), lambda i, c: (i, 0, 0, 0))],
        out_shape=[jax.ShapeDtypeStruct((b, t, H_A * DV_A), F32),
                   jax.ShapeDtypeStruct((b, SUBLANE, QKV_A), F32),
                   jax.ShapeDtypeStruct((b, H_A, DK_A, DV_A), F32)],
        scratch_shapes=[pltpu.VMEM((chunk + SUBLANE, QKV_A), F32),
                        pltpu.VMEM((H_A, DK_A, DV_A), F32)],
        compiler_params=_cparams("parallel", "arbitrary"),
        name="gdn",
    )(proj, proj, proj, conv0, s0, cw, cb.reshape(1, -1), alog.reshape(1, -1), dtb.reshape(1, -1),
      onorm.reshape(1, -1))


def _order_key(score):
    score = jnp.where(score == 0.0, 0.0, score)
    bits = pltpu.bitcast(score, I32)
    return jnp.where(bits < 0, bits ^ jnp.int32(0x7FFFFFFF), bits)


def _kth_largest_key(key, n_sel):
    rows = key.shape[0]

    def count_ge(cand):
        return jnp.sum(jnp.where(key >= cand, 1.0, 0.0), axis=1, keepdims=True)

    int_min = jnp.int32(-2 ** 31)
    zero = jnp.zeros((rows, 1), I32)
    ans = jnp.where(count_ge(zero) >= n_sel, zero, jnp.full((rows, 1), int_min, I32))

    def body(i, ans):
        cand = ans | (jnp.int32(1) << (jnp.int32(30) - i))
        return jnp.where(count_ge(cand) >= n_sel, cand, ans)

    return lax.fori_loop(0, 31, body, ans)


def _select_bias(key, thr, adm, n_sel, ustrict_ref):
    rows, n = key.shape
    need = n_sel - jnp.sum(jnp.where(key > thr, 1.0, 0.0), axis=1, keepdims=True)
    base = jnp.zeros((rows, 1), F32)
    pieces = []
    for c0 in range(0, n, LANE):
        e = jnp.where(key[:, c0:c0 + LANE] == thr, 1.0, 0.0)
        pieces.append(_dot(e.astype(BF16), ustrict_ref[...]) + base)
        base = base + jnp.sum(e, axis=1, keepdims=True)
    rank = jnp.concatenate(pieces, axis=1)
    sel = jnp.where(key > thr, 1.0, jnp.where((key == thr) & (rank < need), 1.0, 0.0))
    return jnp.where(adm & (sel > 0.5), 0.0, NEG)


def _strict_lower_ones(n):
    r = jnp.arange(n)
    return (r[:, None] < r[None, :]).astype(BF16)


def _strict_upper_ones(n):
    r = jnp.arange(n)
    return (r[:, None] > r[None, :]).astype(BF16)


def _dsa_kernel(qi_ref, sm_ref, kit_ref, q_ref, kt_ref, v_ref, us_ref, o_ref, *, tq, n_keys, n_sel, first_block):
    i = pl.program_id(1) + first_block
    q_pos = i * tq + lax.broadcasted_iota(I32, (tq, 1), 0)
    col = lax.broadcasted_iota(I32, (tq, n_keys), 1)
    adm = col <= q_pos
    qi = qi_ref[...].astype(BF16)
    sm = sm_ref[...]
    kit = kit_ref[...]
    score = jnp.zeros((tq, n_keys), F32)
    for h in range(H_IDX):
        rel = jnp.maximum(_dot(qi[:, h * D_IDX:(h + 1) * D_IDX], kit) * (D_IDX ** -0.5), 0.0)
        score = score + rel * (sm[:, 2 * H_A + h:2 * H_A + h + 1] * (H_IDX ** -0.5))
    score = jnp.where(adm, score, NEG)
    key = _order_key(score)
    thr = _kth_largest_key(key, n_sel)
    bias = _select_bias(key, thr, adm, n_sel, us_ref)
    q = q_ref[...]
    v = v_ref[...]
    grp = H_B // HKV_B
    for h in range(H_B):
        g = h // grp
        logits = _dot(q[:, h * DH_B:(h + 1) * DH_B], kt_ref[g * DH_B:(g + 1) * DH_B, :]) * (DH_B ** -0.5) + bias
        m = jnp.max(logits, axis=-1, keepdims=True)
        p = jnp.exp(logits - m)
        s = jnp.sum(p, axis=-1, keepdims=True)
        ov = _dot(p.astype(BF16), v)
        o_ref[:, h * DH_B:(h + 1) * DH_B] = ov[:, g * DH_B:(g + 1) * DH_B] / s


def dsa_prompt(proj, qn, kt, v, kit, *, tq, group):
    b, t, _ = proj.shape
    n_sel = min(TOPK_B, t // 4)
    wq = H_IDX * D_IDX
    us = _strict_lower_ones(LANE)
    outs = []
    for g0 in range(0, t // tq, group):
        n_keys = (g0 + group) * tq
        kern = functools.partial(_dsa_kernel, tq=tq, n_keys=n_keys, n_sel=n_sel, first_block=g0)
        outs.append(pl.pallas_call(
            kern,
            grid=(b, group),
            in_specs=[pl.BlockSpec((None, tq, wq), lambda i, j, g0=g0: (i, j + g0, OFF_QI // wq)),
                      pl.BlockSpec((None, tq, LANE), lambda i, j, g0=g0: (i, j + g0, OFF_SMALL // LANE)),
                      pl.BlockSpec((None, D_IDX, n_keys), lambda i, j: (i, 0, 0)),
                      pl.BlockSpec((None, tq, H_B * DH_B), lambda i, j, g0=g0: (i, j + g0, 0)),
                      pl.BlockSpec((None, HKV_B * DH_B, n_keys), lambda i, j: (i, 0, 0)),
                      pl.BlockSpec((None, n_keys, HKV_B * DH_B), lambda i, j: (i, 0, 0)),
                      pl.BlockSpec((LANE, LANE), lambda i, j: (0, 0))],
            out_specs=pl.BlockSpec((None, tq, H_B * DH_B), lambda i, j: (i, j, 0)),
            out_shape=jax.ShapeDtypeStruct((b, group * tq, H_B * DH_B), F32),
            compiler_params=_cparams("parallel", "parallel"),
            name="dsa_prompt",
        )(proj, proj, kit, qn, kt, v, us))
    return jnp.concatenate(outs, axis=1)


def _dsa_dec_kernel(pt_ref, qi_ref, w_ref, q_ref, knew_ref, vnew_ref, kinew_ref, us_ref, *rest,
                    npp, n_steps, n_past, n_sel, t_new, li):
    kp = rest[0:npp]
    vp = rest[npp:2 * npp]
    kip = rest[2 * npp:3 * npp]
    o_ref, k_all, v_all, ki_all = rest[3 * npp:]
    s = pl.program_id(1)
    page = knew_ref.shape[0]
    for r in range(npp):
        off = pl.multiple_of((s * npp + r) * page, page)
        k_all[pl.ds(off, page), :] = kp[r][...]
        v_all[pl.ds(off, page), :] = vp[r][...]
        ki_all[pl.ds(off, page), :] = kip[r][...]

    @pl.when(s == n_steps - 1)
    def _():
        n_keys = n_past + page
        k_all[n_past:n_keys, :] = knew_ref[...]
        v_all[n_past:n_keys, :] = vnew_ref[...]
        ki_all[n_past:n_keys, :] = kinew_ref[...]
        rows = SUBLANE
        col = lax.broadcasted_iota(I32, (rows, n_keys), 1)
        q_pos = n_past + lax.broadcasted_iota(I32, (rows, 1), 0)
        adm = (col <= q_pos) & (col < n_past + t_new)
        ki = ki_all[...]
        w = w_ref[...]
        rel = jnp.maximum(_dot_nt(qi_ref[...].reshape(H_IDX * rows, LANE), ki, precision=HI) * (D_IDX ** -0.5), 0.0)
        w_rows = jnp.concatenate([w[:, h:h + 1] for h in range(H_IDX)], axis=0) * (H_IDX ** -0.5)
        rel = rel * w_rows
        score = rel[0:rows]
        for h in range(1, H_IDX):
            score = score + rel[h * rows:(h + 1) * rows]
        score = jnp.where(adm, score, NEG)
        key = _order_key(score)
        thr = _kth_largest_key(key, n_sel)
        bias = _select_bias(key, thr, adm, n_sel, us_ref)
        grp = H_B // HKV_B
        bias_g = jnp.concatenate([bias] * grp, axis=0)
        kk = k_all[...]
        vv = v_all[...]
        for g in range(HKV_B):
            logits = _dot3_nt(q_ref[g], kk) * (DH_B ** -0.5) + bias_g
            m = jnp.max(logits, axis=-1, keepdims=True)
            p = jnp.exp(logits - m)
            ssum = jnp.sum(p, axis=-1, keepdims=True)
            og = _dot3(p, vv) / ssum
            for hh in range(grp):
                h = g * grp + hh
                o_ref[:, h * DH_B:(h + 1) * DH_B] = og[hh * rows:(hh + 1) * rows, g * DH_B:(g + 1) * DH_B]


def dsa_decode(page_table, qi_pad, w_idx, q_pad, k_new, v_new, ki_new, pool_k, pool_v, pool_ki, li, *, npp, t_new):
    b, n_pages = page_table.shape
    page = pool_k.shape[1]
    n_steps = n_pages // npp
    n_past = n_pages * page
    n_sel = min(TOPK_B, (n_past + t_new) // 4)
    rows = SUBLANE * (H_B // HKV_B)
    kern = functools.partial(_dsa_dec_kernel, npp=npp, n_steps=n_steps, n_past=n_past, n_sel=n_sel,
                             t_new=t_new, li=li)

    def page_spec(r, lane_block):
        return pl.BlockSpec((None, page, LANE), lambda i, s, pt: (pt[i, s * npp + r], 0, lane_block))

    fixed4 = lambda i, s, pt: (i, 0, 0, 0)
    fixed3 = lambda i, s, pt: (i, 0, 0)
    in_specs = [pl.BlockSpec((None, H_IDX, SUBLANE, LANE), fixed4),
                pl.BlockSpec((None, SUBLANE, H_IDX), fixed3),
                pl.BlockSpec((None, HKV_B, rows, LANE), fixed4),
                pl.BlockSpec((None, page, LANE), fixed3),
                pl.BlockSpec((None, page, LANE), fixed3),
                pl.BlockSpec((None, page, LANE), fixed3),
                pl.BlockSpec((LANE, LANE), lambda i, s, pt: (0, 0))]
    in_specs += [page_spec(r, li) for r in range(npp)]
    in_specs += [page_spec(r, li) for r in range(npp)]
    in_specs += [page_spec(r, 0) for r in range(npp)]
    grid_spec = pltpu.PrefetchScalarGridSpec(
        num_scalar_prefetch=1,
        grid=(b, n_steps),
        in_specs=in_specs,
        out_specs=pl.BlockSpec((None, SUBLANE, H_B * DH_B), fixed3),
        scratch_shapes=[pltpu.VMEM((n_past + page, LANE), F32),
                        pltpu.VMEM((n_past + page, LANE), F32),
                        pltpu.VMEM((n_past + page, LANE), F32)])
    return pl.pallas_call(
        kern,
        grid_spec=grid_spec,
        out_shape=jax.ShapeDtypeStruct((b, SUBLANE, H_B * DH_B), F32),
        compiler_params=_cparams("parallel", "arbitrary"),
        name="dsa_decode",
    )(page_table, qi_pad, w_idx, q_pad, k_new, v_new, ki_new, _strict_lower_ones(LANE),
      *([pool_k] * npp), *([pool_v] * npp), *([pool_ki] * npp))


def _cconv_kernel(glu_ref, conv0_ref, cw_ref, cb_ref, lng_ref, lnb_ref, o_ref, convo_ref, win_ref, *, tt, valid_len):
    halo = 4 * SUBLANE

    @pl.when(pl.program_id(1) == 0)
    def _():
        win_ref[0:halo, :] = conv0_ref[...]

    glu = glu_ref[...]
    win_ref[halo:halo + tt, :] = glu[:, :C_W] * _sigmoid(glu[:, C_W:])
    acc = jnp.broadcast_to(cb_ref[...], (tt, C_W))
    for j in range(CONV_C):
        acc = acc + cw_ref[j:j + 1, :] * win_ref[pl.ds(halo - (CONV_C - 1) + j, tt), :]
    tail = win_ref[pl.ds(valid_len, halo), :]
    win_ref[0:halo, :] = tail
    convo_ref[...] = tail
    mu = jnp.mean(acc, axis=-1, keepdims=True)
    xc = acc - mu
    yn = xc * lax.rsqrt(jnp.mean(xc * xc, axis=-1, keepdims=True) + EPS) * lng_ref[...] + lnb_ref[...]
    o_ref[...] = _silu(yn)


def conformer_conv(proj, conv0, cw, cb, lng, lnb, *, tt, valid_len):
    b, t, _ = proj.shape
    halo = 4 * SUBLANE
    kern = functools.partial(_cconv_kernel, tt=tt, valid_len=valid_len)
    cwp = jnp.pad(cw, ((0, halo - CONV_C), (0, 0)))
    return pl.pallas_call(
        kern,
        grid=(b, t // tt),
        in_specs=[pl.BlockSpec((None, tt, 2 * C_W), lambda i, j: (i, j, 0)),
                  pl.BlockSpec((None, halo, C_W), lambda i, j: (i, 0, 0)),
                  pl.BlockSpec((halo, C_W), lambda i, j: (0, 0)),
                  pl.BlockSpec((1, C_W), lambda i, j: (0, 0)),
                  pl.BlockSpec((1, C_W), lambda i, j: (0, 0)),
                  pl.BlockSpec((1, C_W), lambda i, j: (0, 0))],
        out_specs=[pl.BlockSpec((None, tt, C_W), lambda i, j: (i, j, 0)),
                   pl.BlockSpec((None, halo, C_W), lambda i, j: (i, 0, 0))],
        out_shape=[jax.ShapeDtypeStruct((b, t, C_W), F32),
                   jax.ShapeDtypeStruct((b, halo, C_W), F32)],
        scratch_shapes=[pltpu.VMEM((tt + halo, C_W), F32)],
        compiler_params=_cparams("parallel", "arbitrary"),
        name="conformer_conv",
    )(proj, conv0, cwp, cb.reshape(1, -1), lng.reshape(1, -1), lnb.reshape(1, -1))


F32_EXP_UNDERFLOW = -104.0


def _weights_alive(run):
    return jnp.max(run) > F32_EXP_UNDERFLOW


def _sb_block(q, kj, vj, u_ref, carry, before, scale, precise=False):
    run, acc = carry
    z = _mm_nt(q, kj, precise) * scale
    lk = -_softplus(z)
    if before is not None:
        lk = jnp.where(before, lk, 0.0)
    if precise:
        later = _dot(lk, u_ref[...].astype(F32), precision=HI)
    else:
        hi, lo = _split2(lk)
        later = _dot(hi, u_ref[...]) + _dot(lo, u_ref[...])
    w = jnp.exp(z + lk + later + run)
    if before is not None:
        w = jnp.where(before, w, 0.0)
    acc = acc + _mm(w, vj, precise)
    run = run + later[:, 0:1] + lk[:, 0:1]
    return run, acc


def _sb_kernel(q_ref, k_ref, v_ref, u_ref, o_ref, *, tq):
    i = pl.program_id(2)
    q = q_ref[...]
    scale = DH_D ** -0.5
    rowi = lax.broadcasted_iota(I32, (tq, tq), 0)
    coli = lax.broadcasted_iota(I32, (tq, tq), 1)

    def blk(j, carry, before):
        off = pl.multiple_of(j * tq, tq)
        return _sb_block(q, k_ref[pl.ds(off, tq), :], v_ref[pl.ds(off, tq), :], u_ref, carry, before, scale)

    carry = (jnp.zeros((tq, 1), F32), jnp.zeros((tq, DH_D), F32))
    run, acc = blk(i, carry, coli < rowi)

    def more(c):
        return (c[0] >= 0) & _weights_alive(c[1])

    def step(c):
        run, acc = blk(c[0], (c[1], c[2]), None)
        return c[0] - 1, run, acc

    o_ref[...] = lax.while_loop(more, step, (i - 1, run, acc))[2]


def sb_prompt(q, k, v, *, tq):
    b, h, t, dh = q.shape
    kern = functools.partial(_sb_kernel, tq=tq)
    return pl.pallas_call(
        kern,
        grid=(b, h, t // tq),
        in_specs=[pl.BlockSpec((None, None, tq, dh), lambda i, j, l: (i, j, l, 0)),
                  pl.BlockSpec((None, None, t, dh), lambda i, j, l: (i, j, 0, 0)),
                  pl.BlockSpec((None, None, t, dh), lambda i, j, l: (i, j, 0, 0)),
                  pl.BlockSpec((tq, tq), lambda i, j, l: (0, 0))],
        out_specs=pl.BlockSpec((None, None, tq, dh), lambda i, j, l: (i, j, l, 0)),
        out_shape=jax.ShapeDtypeStruct((b, h, t, dh), F32),
        compiler_params=_cparams("parallel", "parallel", "parallel"),
        name="sb_prompt",
    )(q, k, v, _strict_upper_ones(tq))


def _sb_dec_kernel(pt_ref, qb_ref, knew_ref, vnew_ref, u_ref, kpool_ref, vpool_ref, o_ref, kbuf, vbuf, sem,
                   *, n_pages, li):
    b = pl.program_id(0)
    q = qb_ref[...]
    rows, width = q.shape
    page = knew_ref.shape[0]
    scale = DH_D ** -0.5

    def page_copies(p, slot):
        idx = pt_ref[b, p]
        return (pltpu.make_async_copy(kpool_ref.at[idx, :, li], kbuf.at[slot], sem.at[0, slot]),
                pltpu.make_async_copy(vpool_ref.at[idx, :, li], vbuf.at[slot], sem.at[1, slot]))

    def start(p, slot):
        for c in page_copies(p, slot):
            c.start()

    def wait(p, slot):
        for c in page_copies(p, slot):
            c.wait()

    def slot_of(p):
        return (n_pages - 1 - p) & 1

    def lanes(buf, slot):
        return jnp.concatenate([buf[slot, :, h, :] for h in range(H_D)], axis=1)

    start(n_pages - 1, 0)
    qidx = lax.broadcasted_iota(I32, (rows, page), 0) >> int(math.log2(H_D))
    coli = lax.broadcasted_iota(I32, (rows, page), 1)
    carry = (jnp.zeros((rows, 1), F32), jnp.zeros((rows, width), F32))
    run, acc = _sb_block(q, knew_ref[...], vnew_ref[...], u_ref, carry, coli < qidx, scale, precise=True)

    def more(c):
        return (c[0] >= 0) & _weights_alive(c[1])

    def step(c):
        p = c[0]
        slot = slot_of(p)
        wait(p, slot)

        @pl.when(p > 0)
        def _():
            start(p - 1, 1 - slot)

        run, acc = _sb_block(q, lanes(kbuf, slot), lanes(vbuf, slot), u_ref, (c[1], c[2]), None, scale, precise=True)
        return p - 1, run, acc

    p_end, run, acc = lax.while_loop(more, step, (n_pages - 1, run, acc))

    @pl.when(p_end >= 0)
    def _():
        wait(p_end, slot_of(p_end))

    lane_head = lax.broadcasted_iota(I32, (H_D, width), 1) >> int(math.log2(DH_D))
    own = lane_head == lax.broadcasted_iota(I32, (H_D, width), 0)
    per_q = acc.reshape(rows // H_D, H_D, width)
    o_ref[...] = jnp.sum(jnp.where(own[None], per_q, 0.0), axis=1)


def sb_decode(page_table, qb, k_new, v_new, pool_k, pool_v, li):
    b, n_pages = page_table.shape
    page = pool_k.shape[1]
    rows, width = qb.shape[1], qb.shape[2]
    kern = functools.partial(_sb_dec_kernel, n_pages=n_pages, li=li)
    fixed3 = lambda i, pt: (i, 0, 0)
    grid_spec = pltpu.PrefetchScalarGridSpec(
        num_scalar_prefetch=1,
        grid=(b,),
        in_specs=[pl.BlockSpec((None, rows, width), fixed3),
                  pl.BlockSpec((None, page, width), fixed3),
                  pl.BlockSpec((None, page, width), fixed3),
                  pl.BlockSpec((page, page), lambda i, pt: (0, 0)),
                  pl.BlockSpec(memory_space=pl.ANY),
                  pl.BlockSpec(memory_space=pl.ANY)],
        out_specs=pl.BlockSpec((None, rows // H_D, width), fixed3),
        scratch_shapes=[pltpu.VMEM((2, page, H_D, DH_D), F32), pltpu.VMEM((2, page, H_D, DH_D), F32),
                        pltpu.SemaphoreType.DMA((2, 2))])
    return pl.pallas_call(
        kern,
        grid_spec=grid_spec,
        out_shape=jax.ShapeDtypeStruct((b, rows // H_D, width), F32),
        compiler_params=_cparams("arbitrary"),
        name="sb_decode",
    )(page_table, qb, k_new, v_new, _strict_upper_ones(page), pool_k, pool_v)


def _out_res_kernel(a_ref, b_ref, wa_ref, wb_ref, x_ref, gt_ref, o_ref, *, precise):
    y = _mm(a_ref[...], wa_ref[...], precise) + _mm(b_ref[...], wb_ref[...], precise)
    o_ref[...] = x_ref[...] + gt_ref[...] * y


def out_residual(a, bm, w, li, x, gate, *, tm, precise=False):
    gsz, t, d = x.shape
    ka, kb = a.shape[2], bm.shape[2]
    assert ka == kb
    return pl.pallas_call(
        functools.partial(_out_res_kernel, precise=precise),
        grid=(gsz, t // tm),
        in_specs=[pl.BlockSpec((None, tm, ka), lambda g_, i: (g_, i, 0)),
                  pl.BlockSpec((None, tm, kb), lambda g_, i: (g_, i, 0)),
                  pl.BlockSpec((None, ka, d), lambda g_, i: (li, 0, 0)),
                  pl.BlockSpec((None, kb, d), lambda g_, i: (li, 1, 0)),
                  pl.BlockSpec((None, tm, d), lambda g_, i: (g_, i, 0)),
                  _param_spec(gate.shape[1], tm, d, 2)],
        out_specs=pl.BlockSpec((None, tm, d), lambda g_, i: (g_, i, 0)),
        out_shape=jax.ShapeDtypeStruct((gsz, t, d), F32),
        compiler_params=_cparams("parallel", "parallel"),
        name="out_residual",
    )(a, bm, w, w, x, gate)


def _ffn_kernel(x_ref, g_ref, sh_ref, sc_ref, gt_ref, wg_ref, wu_ref, wd_ref, o_ref, h_ref, acc_ref, *, precise):
    k = pl.program_id(2)

    @pl.when(k == 0)
    def _():
        h_ref[...] = _modulated_norm(x_ref[...], g_ref[...], sh_ref[...], sc_ref[...]).astype(h_ref.dtype)
        acc_ref[...] = jnp.zeros_like(acc_ref)

    h = h_ref[...]
    a = _mm(h, wg_ref[...], precise)
    u = _mm(h, wu_ref[...], precise)
    acc_ref[...] += _mm(_silu(a) * u, wd_ref[...], precise)

    @pl.when(k == pl.num_programs(2) - 1)
    def _():
        o_ref[...] = x_ref[...] + gt_ref[...] * acc_ref[...]


def ffn_residual(x, g, shift, scale, gate, wg, wu, wd, li, *, tm, tf, precise=False):
    gsz, t, d = x.shape
    f = wg.shape[2]
    pspec = _param_spec(shift.shape[1], tm, d, 3)
    return pl.pallas_call(
        functools.partial(_ffn_kernel, precise=precise),
        grid=(gsz, t // tm, f // tf),
        in_specs=[pl.BlockSpec((None, tm, d), lambda g_, i, k: (g_, i, 0)),
                  pl.BlockSpec((1, d), lambda g_, i, k: (0, 0)),
                  pspec, pspec, pspec,
                  pl.BlockSpec((None, d, tf), lambda g_, i, k: (li, 0, k)),
                  pl.BlockSpec((None, d, tf), lambda g_, i, k: (li, 0, k)),
                  pl.BlockSpec((None, tf, d), lambda g_, i, k: (li, k, 0))],
        out_specs=pl.BlockSpec((None, tm, d), lambda g_, i, k: (g_, i, 0)),
        out_shape=jax.ShapeDtypeStruct((gsz, t, d), F32),
        scratch_shapes=[pltpu.VMEM((tm, d), _act_dtype(precise)), pltpu.VMEM((tm, d), F32)],
        compiler_params=_cparams("parallel", "parallel", "arbitrary"),
        name="ffn_residual",
    )(x, g, shift, scale, gate, wg, wu, wd)


def _router_kernel(x_ref, g_ref, sh_ref, sc_ref, wr_ref, h_ref, r_ref):
    h = _modulated_norm(x_ref[...], g_ref[...], sh_ref[...], sc_ref[...])
    h_ref[...] = h
    logits = _dot(h, wr_ref[...], precision=HI)
    lane = lax.broadcasted_iota(I32, logits.shape, 1)
    lanef = lane.astype(F32)
    logits = jnp.where(lane < N_EXP, logits, -jnp.inf)
    v1 = jnp.max(logits, axis=-1, keepdims=True)
    i1 = jnp.min(jnp.where(logits == v1, lanef, float(LANE)), axis=-1, keepdims=True)
    rest = jnp.where(lanef == i1, -jnp.inf, logits)
    v2 = jnp.max(rest, axis=-1, keepdims=True)
    i2 = jnp.min(jnp.where(rest == v2, lanef, float(LANE)), axis=-1, keepdims=True)
    e2 = jnp.exp(v2 - v1)
    p1 = 1.0 / (1.0 + e2)
    p2 = e2 / (1.0 + e2)
    r_ref[...] = jnp.where(lane == 0, i1, jnp.where(lane == 1, i2, jnp.where(lane == 2, p1, jnp.where(lane == 3, p2, 0.0))))


def moe_router(x, g, shift, scale, w_router, li, *, tm):
    gsz, t, d = x.shape
    wr = jnp.pad(w_router[li], ((0, 0), (0, LANE - N_EXP)))
    pspec = _param_spec(shift.shape[1], tm, d, 2)
    return pl.pallas_call(
        _router_kernel,
        grid=(gsz, t // tm),
        in_specs=[pl.BlockSpec((None, tm, d), lambda g_, i: (g_, i, 0)),
                  pl.BlockSpec((1, d), lambda g_, i: (0, 0)),
                  pspec, pspec,
                  pl.BlockSpec((d, LANE), lambda g_, i: (0, 0))],
        out_specs=[pl.BlockSpec((None, tm, d), lambda g_, i: (g_, i, 0)),
                   pl.BlockSpec((None, tm, LANE), lambda g_, i: (g_, i, 0))],
        out_shape=[jax.ShapeDtypeStruct((gsz, t, d), F32),
                   jax.ShapeDtypeStruct((gsz, t, LANE), F32)],
        compiler_params=_cparams("parallel", "parallel"),
        name="moe_router",
    )(x, g, shift, scale, wr)


def _row_copy(src_ref, dst_ref, src_row, dst_row, sem):
    return pltpu.make_async_copy(src_ref.at[pl.ds(src_row, 1), :], dst_ref.at[pl.ds(dst_row, 1), :], sem)


def _gather_kernel(idx_ref, src_ref, o_ref, sem, *, tg):
    base = pl.program_id(0) * tg

    def issue(r, carry):
        _row_copy(src_ref, o_ref, idx_ref[base + r], r, sem).start()
        return carry

    lax.fori_loop(0, tg, issue, 0)

    def drain(r, carry):
        _row_copy(src_ref, o_ref, 0, r, sem).wait()
        return carry

    lax.fori_loop(0, tg, drain, 0)


def gather_rows(src, idx, *, tg):
    p = idx.shape[0]
    d = src.shape[1]
    grid_spec = pltpu.PrefetchScalarGridSpec(
        num_scalar_prefetch=1,
        grid=(p // tg,),
        in_specs=[pl.BlockSpec(memory_space=pl.ANY)],
        out_specs=pl.BlockSpec((tg, d), lambda i, idx_: (i, 0)),
        scratch_shapes=[pltpu.SemaphoreType.DMA(())])
    return pl.pallas_call(
        functools.partial(_gather_kernel, tg=tg),
        grid_spec=grid_spec,
        out_shape=jax.ShapeDtypeStruct((p, d), F32),
        compiler_params=_cparams("arbitrary"),
        name="gather_rows",
    )(idx, src)


def _expert_kernel(te_ref, act_ref, xs_ref, wg_ref, wu_ref, wd_ref, o_ref, h_ref, acc_ref, *, precise):
    i = pl.program_id(0)
    k = pl.program_id(1)
    nk = pl.num_programs(1)
    active = act_ref[i] == 1

    @pl.when(active & (k == 0))
    def _():
        h_ref[...] = xs_ref[...].astype(h_ref.dtype)
        acc_ref[...] = jnp.zeros_like(acc_ref)

    @pl.when(active)
    def _():
        h = h_ref[...]
        a = _mm(h, wg_ref[...], precise)
        u = _mm(h, wu_ref[...], precise)
        acc_ref[...] += _mm(_silu(a) * u, wd_ref[...], precise)

    @pl.when(active & (k == nk - 1))
    def _():
        o_ref[...] = acc_ref[...]

    @pl.when(jnp.logical_not(active) & (k == nk - 1))
    def _():
        o_ref[...] = jnp.zeros_like(o_ref)


def moe_experts(xs, tile_expert, tile_active, wg, wu, wd, li, *, tm, tf, precise=False):
    p, d = xs.shape
    f = wg.shape[3]
    nk = f // tf

    def kk(i, k, act):
        return jnp.where(act[i] == 1, k, nk - 1)

    grid_spec = pltpu.PrefetchScalarGridSpec(
        num_scalar_prefetch=2,
        grid=(p // tm, nk),
        in_specs=[pl.BlockSpec((tm, d), lambda i, k, te, act: (i, 0)),
                  pl.BlockSpec((None, None, d, tf), lambda i, k, te, act: (li, te[i], 0, kk(i, k, act))),
                  pl.BlockSpec((None, None, d, tf), lambda i, k, te, act: (li, te[i], 0, kk(i, k, act))),
                  pl.BlockSpec((None, None, tf, d), lambda i, k, te, act: (li, te[i], kk(i, k, act), 0))],
        out_specs=pl.BlockSpec((tm, d), lambda i, k, te, act: (i, 0)),
        scratch_shapes=[pltpu.VMEM((tm, d), _act_dtype(precise)), pltpu.VMEM((tm, d), F32)])
    return pl.pallas_call(
        functools.partial(_expert_kernel, precise=precise),
        grid_spec=grid_spec,
        out_shape=jax.ShapeDtypeStruct((p, d), F32),
        compiler_params=_cparams("arbitrary", "arbitrary"),
        name="moe_experts",
    )(tile_expert, tile_active, xs, wg, wu, wd)


def _combine_kernel(p0_ref, p1_ref, ys_ref, x_ref, gt_ref, r_ref, o_ref, a_ref, b_ref, sem, *, tc, t):
    base = pl.program_id(0) * t + pl.program_id(1) * tc

    def issue(r, carry):
        _row_copy(ys_ref, a_ref, p0_ref[base + r], r, sem.at[0]).start()
        _row_copy(ys_ref, b_ref, p1_ref[base + r], r, sem.at[1]).start()
        return carry

    lax.fori_loop(0, tc, issue, 0)

    def drain(r, carry):
        _row_copy(ys_ref, a_ref, 0, r, sem.at[0]).wait()
        _row_copy(ys_ref, b_ref, 0, r, sem.at[1]).wait()
        return carry

    lax.fori_loop(0, tc, drain, 0)
    r = r_ref[...]
    o_ref[...] = x_ref[...] + gt_ref[...] * (r[:, 2:3] * a_ref[...] + r[:, 3:4] * b_ref[...])


def moe_combine(ys, pos0, pos1, x, gate, route, *, tc):
    gsz, t, d = x.shape
    if gate.shape[1] == 1:
        gspec = pl.BlockSpec((None, 1, d), lambda g_, i, a, b: (g_, 0, 0))
    else:
        gspec = pl.BlockSpec((None, tc, d), lambda g_, i, a, b: (g_, i, 0))
    grid_spec = pltpu.PrefetchScalarGridSpec(
        num_scalar_prefetch=2,
        grid=(gsz, t // tc),
        in_specs=[pl.BlockSpec(memory_space=pl.ANY),
                  pl.BlockSpec((None, tc, d), lambda g_, i, a, b: (g_, i, 0)),
                  gspec,
                  pl.BlockSpec((None, tc, LANE), lambda g_, i, a, b: (g_, i, 0))],
        out_specs=pl.BlockSpec((None, tc, d), lambda g_, i, a, b: (g_, i, 0)),
        scratch_shapes=[pltpu.VMEM((tc, d), F32), pltpu.VMEM((tc, d), F32), pltpu.SemaphoreType.DMA((2,))])
    return pl.pallas_call(
        functools.partial(_combine_kernel, tc=tc, t=t),
        grid_spec=grid_spec,
        out_shape=jax.ShapeDtypeStruct((gsz, t, d), F32),
        compiler_params=_cparams("arbitrary", "arbitrary"),
        name="moe_combine",
    )(pos0, pos1, ys, x, gate, route)


def _routing_tables(idx, tm):
    m = idx.shape[0]
    e_flat = idx.reshape(-1)
    onehot = (e_flat[:, None] == jnp.arange(N_EXP, dtype=I32)[None, :]).astype(I32)
    rank = jnp.take_along_axis(jnp.cumsum(onehot, axis=0) - onehot, e_flat[:, None], axis=1)[:, 0]
    counts = jnp.sum(onehot, axis=0)
    padded = ((counts + tm - 1) // tm) * tm
    starts = jnp.cumsum(padded) - padded
    pos = starts[e_flat] + rank
    n_rows = ((2 * m + N_EXP * (tm - 1)) // tm) * tm
    n_tiles = n_rows // tm
    row_token = jnp.zeros((n_rows,), I32).at[pos].set(jnp.arange(2 * m, dtype=I32) // 2)
    tile_start = jnp.arange(n_tiles, dtype=I32) * tm
    ends = jnp.cumsum(padded)
    tile_e = jnp.sum((tile_start[:, None] >= ends[None, :]).astype(I32), axis=1)
    total = ends[-1]
    active = (tile_start < total).astype(I32)
    last_e = jnp.max(jnp.where(counts > 0, jnp.arange(N_EXP, dtype=I32), 0))
    tile_e = jnp.where(active == 1, jnp.minimum(tile_e, N_EXP - 1), last_e).astype(I32)
    return row_token, tile_e, active, pos.reshape(m, 2)


def moe_residual(x, g, shift, scale, gate, w_router, wg, wu, wd, li, *, tm_tok, tm, tf, precise=False):
    gsz, t, d = x.shape
    h, r = moe_router(x, g, shift, scale, w_router, li, tm=tm_tok)
    idx = r.reshape(gsz * t, LANE)[:, 0:2].astype(I32)
    row_token, tile_e, active, pos = _routing_tables(idx, tm)
    xs = gather_rows(h.reshape(gsz * t, d), row_token, tg=min(tm, 256))
    ys = moe_experts(xs, tile_e, active, wg, wu, wd, li, tm=tm, tf=tf, precise=precise)
    return moe_combine(ys, pos[:, 0], pos[:, 1], x, gate, r, tc=min(tm_tok, 256))


def _pad_ab_weight(w):
    s = [QKV_A, H_A * DV_A, H_A, H_A, H_B * DH_B, HKV_B * DH_B, HKV_B * DH_B, H_IDX * D_IDX, D_IDX, H_IDX]
    o = [0]
    for n in s:
        o.append(o[-1] + n)
    qkv, z, a, b, qb, kb, vb, qi, ki, wi = (w[:, :, o[j]:o[j + 1]] for j in range(10))
    zeros = lambda n: jnp.zeros(w.shape[:2] + (n,), w.dtype)
    return jnp.concatenate([qkv, z, qb, kb, vb, qi, ki, zeros(LANE - D_IDX), a, b, wi, zeros(LANE - 2 * H_A - H_IDX)], axis=-1)


def _tok_tile(t, want):
    return want if t % want == 0 else t


def _trunk(x, mods, w, past, ab_w_pad, tokens_per_seq):
    depth = w['w_ada'].shape[0]
    gsz, t, d = x.shape
    nseq = gsz * t // tokens_per_seq
    ts = tokens_per_seq
    decode = past is not None
    new = {name: [] for name in ('a_s', 'a_conv', 'b_k', 'b_v', 'b_kidx', 'c_conv', 'd_k', 'd_v')}
    tm = _tok_tile(t, 512)

    def seq(a):
        return a.reshape(nseq, ts, a.shape[-1])

    def tok(a):
        return a.reshape(gsz, t, a.shape[-1])

    for layer in range(depth):
        li = layer // 2
        shift1, scale1, gate1, shift2, scale2, gate2 = mods[layer]
        g_mix = w['norm_mix'][layer].reshape(1, d)
        g_ffn = w['norm_ffn'][layer].reshape(1, d)
        if layer % 2 == 0:
            proj = mod_linear(x, g_mix, shift1, scale1, ab_w_pad, li, tm=tm, tn=AB_COLS // 2, precise=decode)
            qn = head_norm(proj, OFF_QB // 512, 512, w['ab_qnorm'][li], DH_B, tm=tm)
            kn = head_norm(proj, OFF_KV // 128, 128, w['ab_knorm'][li], DH_B, tm=tm)
            projs = seq(proj)
            v_b = projs[:, :, OFF_KV + 128:OFF_KV + 256]
            k_i = projs[:, :, OFF_KI:OFF_KI + D_IDX]
            kns = seq(kn)
            if not decode:
                conv0 = jnp.zeros((nseq, SUBLANE, QKV_A), F32)
                s0 = jnp.zeros((nseq, H_A, DK_A, DV_A), F32)
                o_a, conv_t, s_new = gdn(projs, conv0, s0, w['ab_conv_w'][li], w['ab_conv_b'][li], w['ab_a_log'][li],
                                         w['ab_dt_bias'][li], w['ab_onorm'][li], chunk=CHUNK_A, valid_len=CHUNK_A)
                o_b = dsa_prompt(projs, seq(qn).astype(BF16), jnp.swapaxes(kns, 1, 2).astype(BF16),
                                 v_b.astype(BF16), jnp.swapaxes(k_i, 1, 2).astype(BF16), tq=128,
                                 group=math.gcd(4, ts // 128))
            else:
                cpad = SUBLANE
                conv0 = jnp.pad(past['a_conv'][li], ((0, 0), (SUBLANE - (CONV_A - 1), 0), (0, 0)))
                proj_p = jnp.pad(projs, ((0, 0), (0, cpad - ts), (0, 0)))
                o_a, conv_t, s_new = gdn(proj_p, conv0, past['a_s'][li], w['ab_conv_w'][li], w['ab_conv_b'][li],
                                         w['ab_a_log'][li], w['ab_dt_bias'][li], w['ab_onorm'][li],
                                         chunk=cpad, valid_len=ts, precise=True)
                o_a = o_a[:, :ts]
                o_b = _dsa_decode_wrap(projs, seq(qn), kns, v_b, k_i, past, li, ts)
            new['a_s'].append(s_new)
            new['a_conv'].append(conv_t[:, SUBLANE - (CONV_A - 1):])
            new['b_k'].append(kns.reshape(nseq, ts, HKV_B, DH_B))
            new['b_v'].append(v_b.reshape(nseq, ts, HKV_B, DH_B))
            new['b_kidx'].append(k_i)
            x = out_residual(tok(o_a), tok(o_b), w['ab_w_out'], li, x, gate1, tm=tm, precise=decode)
            x = ffn_residual(x, g_ffn, shift2, scale2, gate2, w['ff_w_gate'], w['ff_w_up'], w['ff_w_down'], li,
                             tm=_tok_tile(t, 1024), tf=256, precise=decode)
        else:
            proj = mod_linear(x, g_mix, shift1, scale1, w['cd_w_in'], li, tm=tm, tn=1280, precise=decode)
            qn = head_norm(proj, 2, 512, w['cd_qnorm'][li], DH_D, tm=tm)
            kn = head_norm(proj, 3, 512, w['cd_knorm'][li], DH_D, tm=tm)
            projs = seq(proj)
            v_d = projs[:, :, 4 * C_W:5 * C_W]
            kns, qns = seq(kn), seq(qn)
            halo = 4 * SUBLANE
            if not decode:
                conv0 = jnp.zeros((nseq, halo, C_W), F32)
                o_c, conv_t = conformer_conv(projs, conv0, w['cd_conv_w'][li], w['cd_conv_b'][li], w['cd_ln_g'][li],
                                             w['cd_ln_b'][li], tt=512, valid_len=512)
                heads = lambda a: a.reshape(nseq, ts, H_D, DH_D).transpose(0, 2, 1, 3).astype(BF16)
                o_d = sb_prompt(heads(qns), heads(kns), heads(v_d), tq=256)
                o_d = o_d.transpose(0, 2, 1, 3).reshape(nseq, ts, H_D * DH_D)
            else:
                conv0 = jnp.pad(past['c_conv'][li], ((0, 0), (halo - (CONV_C - 1), 0), (0, 0)))
                proj_p = jnp.pad(projs, ((0, 0), (0, SUBLANE - ts), (0, 0)))
                o_c, conv_t = conformer_conv(proj_p, conv0, w['cd_conv_w'][li], w['cd_conv_b'][li], w['cd_ln_g'][li],
                                             w['cd_ln_b'][li], tt=SUBLANE, valid_len=ts)
                o_c = o_c[:, :ts]
                o_d = _sb_decode_wrap(qns, kns, v_d, past, li, ts)
            new['c_conv'].append(conv_t[:, halo - (CONV_C - 1):])
            new['d_k'].append(kns.reshape(nseq, ts, H_D, DH_D))
            new['d_v'].append(v_d.reshape(nseq, ts, H_D, DH_D))
            x = out_residual(tok(o_c), tok(o_d), w['cd_w_out'], li, x, gate1, tm=tm, precise=decode)
            x = moe_residual(x, g_ffn, shift2, scale2, gate2, w['moe_router'], w['moe_w_gate'], w['moe_w_up'],
                             w['moe_w_down'], li, tm_tok=tm, tm=tm, tf=512, precise=decode)
    stacked = (jnp.stack(new['a_s'], 0), jnp.stack(new['a_conv'], 0),
               jnp.stack(new['b_k'], 2), jnp.stack(new['b_v'], 2), jnp.stack(new['b_kidx'], 2),
               jnp.stack(new['c_conv'], 0), jnp.stack(new['d_k'], 2), jnp.stack(new['d_v'], 2))
    return x, stacked


def _dsa_decode_wrap(projs, qns, kns, v_b, k_i, past, li, ts):
    nseq = projs.shape[0]
    page = past['b_k'].shape[1]
    n_layers = past['b_k'].shape[2]
    grp = H_B // HKV_B
    pad_rows = lambda a, n: jnp.pad(a, ((0, 0), (0, n - a.shape[1]), (0, 0)))
    qi = projs[:, :, OFF_QI:OFF_QI + H_IDX * D_IDX].reshape(nseq, ts, H_IDX, D_IDX).transpose(0, 2, 1, 3)
    qi = jnp.pad(qi, ((0, 0), (0, 0), (0, SUBLANE - ts), (li * D_IDX, LANE - (li + 1) * D_IDX)))
    w_idx = pad_rows(projs[:, :, OFF_SMALL + 2 * H_A:OFF_SMALL + 2 * H_A + H_IDX], SUBLANE)
    q = qns.reshape(nseq, ts, HKV_B, grp, DH_B).transpose(0, 2, 3, 1, 4)
    q = jnp.pad(q, ((0, 0), (0, 0), (0, 0), (0, SUBLANE - ts), (0, 0)))
    q = q.reshape(nseq, HKV_B, grp * SUBLANE, DH_B)
    q = jnp.stack([jnp.pad(q[:, g], ((0, 0), (0, 0), (g * DH_B, LANE - (g + 1) * DH_B))) for g in range(HKV_B)], 1)
    ki_new = jnp.pad(k_i, ((0, 0), (0, page - ts), (li * D_IDX, LANE - (li + 1) * D_IDX)))
    o = dsa_decode(past['page_table'], qi, w_idx, q, pad_rows(kns, page), pad_rows(v_b, page), ki_new,
                   past['b_k'].reshape(-1, page, n_layers * HKV_B * DH_B),
                   past['b_v'].reshape(-1, page, n_layers * HKV_B * DH_B),
                   past['b_kidx'].reshape(-1, page, n_layers * D_IDX), li,
                   npp=math.gcd(PAGES_PER_STEP, past['page_table'].shape[1]), t_new=ts)
    return o[:, :ts]


def _sb_decode_wrap(qns, kns, v_d, past, li, ts):
    nseq = qns.shape[0]
    page = past['d_k'].shape[1]
    n_layers = past['d_k'].shape[2]
    width = H_D * DH_D
    q = qns.reshape(nseq, ts, H_D, DH_D)
    eye = jnp.eye(H_D, dtype=F32)
    qb = (q[:, :, :, None, :] * eye[None, None, :, :, None]).reshape(nseq, ts * H_D, width)
    pad_rows = lambda a: jnp.pad(a, ((0, 0), (0, page - ts), (0, 0)))
    return sb_decode(past['page_table'], qb, pad_rows(kns), pad_rows(v_d), past['d_k'], past['d_v'], li)


def kernel(x_prompt, x_sample, state_a_s, state_a_conv, cache_b_k, cache_b_v, cache_b_kidx, state_c_conv, cache_d_k, cache_d_v, page_table, c_prompt, c_sample, w_ada, b_ada, norm_mix, norm_ffn, ab_w_in, ab_conv_w, ab_conv_b, ab_a_log, ab_dt_bias, ab_onorm, ab_qnorm, ab_knorm, ab_w_out, cd_w_in, cd_conv_w, cd_conv_b, cd_ln_g, cd_ln_b, cd_qnorm, cd_knorm, cd_w_out, ff_w_gate, ff_w_up, ff_w_down, moe_router, moe_w_gate, moe_w_up, moe_w_down):
    w = {'w_ada': w_ada, 'b_ada': b_ada, 'norm_mix': norm_mix, 'norm_ffn': norm_ffn,
         'ab_conv_w': ab_conv_w, 'ab_conv_b': ab_conv_b, 'ab_a_log': ab_a_log,
         'ab_dt_bias': ab_dt_bias, 'ab_onorm': ab_onorm, 'ab_qnorm': ab_qnorm, 'ab_knorm': ab_knorm,
         'ab_w_out': ab_w_out, 'cd_w_in': cd_w_in, 'cd_conv_w': cd_conv_w, 'cd_conv_b': cd_conv_b,
         'cd_ln_g': cd_ln_g, 'cd_ln_b': cd_ln_b, 'cd_qnorm': cd_qnorm, 'cd_knorm': cd_knorm,
         'cd_w_out': cd_w_out, 'ff_w_gate': ff_w_gate, 'ff_w_up': ff_w_up, 'ff_w_down': ff_w_down,
         'moe_router': moe_router, 'moe_w_gate': moe_w_gate, 'moe_w_up': moe_w_up, 'moe_w_down': moe_w_down}
    past = {'a_s': state_a_s, 'a_conv': state_a_conv, 'b_k': cache_b_k, 'b_v': cache_b_v,
            'b_kidx': cache_b_kidx, 'c_conv': state_c_conv, 'd_k': cache_d_k, 'd_v': cache_d_v,
            'page_table': page_table}
    bp, tp, d = x_prompt.shape
    bs, tsq, _ = x_sample.shape
    depth = w_ada.shape[0]
    rows = bp + bs
    rpad = -rows % SUBLANE
    c_all = jnp.pad(jnp.concatenate([c_prompt, c_sample], axis=0), ((0, rpad), (0, 0)))
    mod = ada_ln(c_all, w_ada, b_ada)
    mod = mod.reshape(depth, rows + rpad, 6, d)
    mods_p = [[mod[l, :bp, j].reshape(bp, 1, d) for j in range(6)] for l in range(depth)]
    mods_s = [[jnp.repeat(mod[l, bp:bp + bs, j], tsq, axis=0).reshape(1, bs * tsq, d) for j in range(6)]
              for l in range(depth)]
    ab_w_pad = _pad_ab_weight(ab_w_in)
    y_p, (pa_s, pa_conv, pb_k, pb_v, pb_kidx, pc_conv, pd_k, pd_v) = _trunk(x_prompt, mods_p, w, None, ab_w_pad, tp)
    y_s, (sa_s, sa_conv, sb_k, sb_v, sb_kidx, sc_conv, sd_k, sd_v) = _trunk(
        x_sample.reshape(1, bs * tsq, d), mods_s, w, past, ab_w_pad, tsq)
    return (y_p, y_s.reshape(bs, tsq, d), pa_s, sa_s, pa_conv, sa_conv, pb_k, sb_k, pb_v, sb_v, pb_kidx, sb_kidx,
            pc_conv, sc_conv, pd_k, sd_k, pd_v, sd_v)
```

```python
import functools
import math

import jax
import jax.numpy as jnp
from jax import lax
from jax.experimental import pallas as pl
from jax.experimental.pallas import tpu as pltpu

F32 = jnp.float32
BF16 = jnp.bfloat16
I32 = jnp.int32
HI = lax.Precision.HIGHEST

EPS = 1e-6
H_A, DK_A, DV_A, CONV_A, CHUNK_A = 4, 128, 128, 4, 64
QKV_A = H_A * (2 * DK_A + DV_A)
H_B, HKV_B, DH_B, H_IDX, D_IDX, TOPK_B = 8, 2, 64, 4, 64, 256
C_W, CONV_C = 512, 31
H_D, DH_D = 8, 64
N_EXP, TOP_E = 8, 2
NEG = -1e30
PAGES_PER_STEP = 8
ROW_DMA_UNROLL = 8

SUBLANE, LANE = 8, 128
VMEM_LIMIT = 56 * 1024 * 1024

AB_COLS = 3328
OFF_QKV, OFF_Z, OFF_QB, OFF_KV, OFF_QI, OFF_KI, OFF_SMALL = 0, 1536, 2048, 2560, 2816, 3072, 3200


def _cparams(*sem):
    return pltpu.CompilerParams(dimension_semantics=sem, vmem_limit_bytes=VMEM_LIMIT)


def _sigmoid(x):
    return 1.0 / (1.0 + jnp.exp(-x))


def _silu(x):
    return x * _sigmoid(x)


def _softplus(x):
    return jnp.maximum(x, 0.0) + jnp.log(1.0 + jnp.exp(-jnp.abs(x)))


def _dot(a, b, **kw):
    return jnp.dot(a, b, preferred_element_type=F32, **kw)


def _dot_nt(a, b, **kw):
    return lax.dot_general(a, b, (((1,), (1,)), ((), ())), preferred_element_type=F32, **kw)


def _dot_tn(a, b, **kw):
    return lax.dot_general(a, b, (((0,), (0,)), ((), ())), preferred_element_type=F32, **kw)


def _split2(a):
    hi = a.astype(BF16)
    lo = (a - hi.astype(F32)).astype(BF16)
    return hi, lo


def _dot3(a, b, precise=False):
    if precise:
        return _dot(a, b, precision=HI)
    ah, al = _split2(a)
    bh, bl = _split2(b)
    return _dot(ah, bh) + (_dot(ah, bl) + _dot(al, bh))


def _dot3_nt(a, b, precise=False):
    if precise:
        return _dot_nt(a, b, precision=HI)
    ah, al = _split2(a)
    bh, bl = _split2(b)
    return _dot_nt(ah, bh) + (_dot_nt(ah, bl) + _dot_nt(al, bh))


def _mm(a, b, precise):
    if precise:
        return _dot(a.astype(F32), b.astype(F32), precision=HI)
    return _dot(a.astype(BF16), b.astype(BF16))


def _mm_nt(a, b, precise):
    if precise:
        return _dot_nt(a.astype(F32), b.astype(F32), precision=HI)
    return _dot_nt(a.astype(BF16), b.astype(BF16))


def _mm_tn(a, b, precise):
    if precise:
        return _dot_tn(a.astype(F32), b.astype(F32), precision=HI)
    return _dot_tn(a.astype(BF16), b.astype(BF16))


def _act_dtype(precise):
    return F32 if precise else BF16


def _modulated_norm(x, g, shift, scale):
    ms = jnp.mean(x * x, axis=-1, keepdims=True)
    return (x * lax.rsqrt(ms + EPS) * g) * (1.0 + scale) + shift


def _param_spec(r, tm, d, ngrid):
    if r == 1:
        if ngrid == 3:
            return pl.BlockSpec((None, 1, d), lambda g, i, j: (g, 0, 0))
        return pl.BlockSpec((None, 1, d), lambda g, i: (g, 0, 0))
    if ngrid == 3:
        return pl.BlockSpec((None, tm, d), lambda g, i, j: (g, i, 0))
    return pl.BlockSpec((None, tm, d), lambda g, i: (g, i, 0))


def _ada_kernel(c_ref, w_ref, b_ref, o_ref):
    a = _silu(c_ref[...])
    o_ref[...] = _dot(a, w_ref[...], precision=HI) + b_ref[...]


def ada_ln(c, w_ada, b_ada, tn=1536):
    r, d = c.shape
    depth, _, n = w_ada.shape
    return pl.pallas_call(
        _ada_kernel,
        grid=(depth, n // tn),
        in_specs=[pl.BlockSpec((r, d), lambda l, j: (0, 0)),
                  pl.BlockSpec((None, d, tn), lambda l, j: (l, 0, j)),
                  pl.BlockSpec((None, 1, tn), lambda l, j: (l, 0, j))],
        out_specs=pl.BlockSpec((None, r, tn), lambda l, j: (l, 0, j)),
        out_shape=jax.ShapeDtypeStruct((depth, r, n), F32),
        compiler_params=_cparams("parallel", "parallel"),
        name="ada_ln",
    )(c, w_ada, b_ada.reshape(depth, 1, n))


def _mod_linear_kernel(x_ref, g_ref, sh_ref, sc_ref, w_ref, o_ref, h_ref, *, precise):
    @pl.when(pl.program_id(2) == 0)
    def _():
        h_ref[...] = _modulated_norm(x_ref[...], g_ref[...], sh_ref[...], sc_ref[...]).astype(h_ref.dtype)

    o_ref[...] = _mm(h_ref[...], w_ref[...], precise)


def mod_linear(x, g, shift, scale, w, li, *, tm, tn, precise=False):
    gsz, t, d = x.shape
    n = w.shape[2]
    pspec = _param_spec(shift.shape[1], tm, d, 3)
    return pl.pallas_call(
        functools.partial(_mod_linear_kernel, precise=precise),
        grid=(gsz, t // tm, n // tn),
        in_specs=[pl.BlockSpec((None, tm, d), lambda g_, i, j: (g_, i, 0)),
                  pl.BlockSpec((1, d), lambda g_, i, j: (0, 0)),
                  pspec, pspec,
                  pl.BlockSpec((None, d, tn), lambda g_, i, j: (li, 0, j))],
        out_specs=pl.BlockSpec((None, tm, tn), lambda g_, i, j: (g_, i, j)),
        out_shape=jax.ShapeDtypeStruct((gsz, t, n), F32),
        scratch_shapes=[pltpu.VMEM((tm, d), _act_dtype(precise))],
        compiler_params=_cparams("parallel", "parallel", "arbitrary"),
        name="mod_linear",
    )(x, g, shift, scale, w)


def _head_norm_kernel(x_ref, g_ref, bd_ref, o_ref):
    x = x_ref[...]
    ms = _dot(x * x, bd_ref[...], precision=HI)
    o_ref[...] = x * lax.rsqrt(ms + EPS) * g_ref[...]


def head_norm(x, col_block, width, g, dh, *, tm):
    gsz, t, _ = x.shape
    idx = jnp.arange(width) // dh
    bd = (idx[:, None] == idx[None, :]).astype(F32) / dh
    gt = jnp.tile(g.astype(F32), width // dh).reshape(1, width)
    return pl.pallas_call(
        _head_norm_kernel,
        grid=(gsz, t // tm),
        in_specs=[pl.BlockSpec((None, tm, width), lambda g_, i: (g_, i, col_block)),
                  pl.BlockSpec((1, width), lambda g_, i: (0, 0)),
                  pl.BlockSpec((width, width), lambda g_, i: (0, 0))],
        out_specs=pl.BlockSpec((None, tm, width), lambda g_, i: (g_, i, 0)),
        out_shape=jax.ShapeDtypeStruct((gsz, t, width), F32),
        compiler_params=_cparams("parallel", "parallel"),
        name="head_norm",
    )(x, gt, bd)


def _gdn_kernel(qkv_ref, z_ref, ab_ref, conv0_ref, s0_ref, cw_ref, cb_ref, alog_ref, dtb_ref, onorm_ref,
                o_ref, convo_ref, so_ref, win_ref, s_ref, *, chunk, valid_len, precise):
    c = pl.program_id(1)
    cc = chunk
    hc = H_A * cc
    tail_rows = SUBLANE

    @pl.when(c == 0)
    def _():
        win_ref[0:tail_rows, :] = conv0_ref[...]
        s_ref[...] = s0_ref[...]

    win_ref[tail_rows:tail_rows + cc, :] = qkv_ref[...]
    acc = jnp.broadcast_to(cb_ref[...], (cc, QKV_A))
    for j in range(CONV_A):
        acc = acc + cw_ref[j:j + 1, :] * win_ref[pl.ds(tail_rows - (CONV_A - 1) + j, cc), :]
    y = _silu(acc)
    tail = win_ref[pl.ds(valid_len, tail_rows), :]
    win_ref[0:tail_rows, :] = tail
    convo_ref[...] = tail

    def l2n(a):
        return a * lax.rsqrt(jnp.sum(a * a, axis=-1, keepdims=True) + EPS)

    def stack(off, width):
        return jnp.concatenate([y[:, off + h * width: off + (h + 1) * width] for h in range(H_A)], axis=0)

    q = l2n(stack(0, DK_A)) * (DK_A ** -0.5)
    k = l2n(stack(H_A * DK_A, DK_A))
    v = stack(2 * H_A * DK_A, DV_A)
    ab = ab_ref[...]
    a_raw = jnp.concatenate([ab[:, h:h + 1] for h in range(H_A)], axis=0)
    b_raw = jnp.concatenate([ab[:, H_A + h:H_A + h + 1] for h in range(H_A)], axis=0)
    alog = jnp.concatenate([jnp.broadcast_to(alog_ref[:, h:h + 1], (cc, 1)) for h in range(H_A)], axis=0)
    dtb = jnp.concatenate([jnp.broadcast_to(dtb_ref[:, h:h + 1], (cc, 1)) for h in range(H_A)], axis=0)
    g_col = -jnp.exp(alog) * _softplus(a_raw + dtb)
    beta = _sigmoid(b_raw)
    rowi = lax.broadcasted_iota(I32, (hc, hc), 0)
    coli = lax.broadcasted_iota(I32, (hc, hc), 1)
    sh = int(math.log2(cc))
    if valid_len < cc:
        rv = (lax.broadcasted_iota(I32, (hc, 1), 0) & (cc - 1)) < valid_len
        g_col = jnp.where(rv, g_col, 0.0)
        beta = jnp.where(rv, beta, 0.0)
    same = (rowi >> sh) == (coli >> sh)
    eye = rowi == coli
    incl = same & (rowi >= coli)
    strict = same & (rowi > coli)

    def to_row(col):
        return jnp.sum(jnp.where(eye, col, 0.0), axis=0, keepdims=True)

    gc_col = jnp.sum(jnp.where(incl, to_row(g_col), 0.0), axis=1, keepdims=True)
    gc_row = to_row(gc_col)
    decay = jnp.where(incl, jnp.exp(jnp.where(incl, gc_col - gc_row, 0.0)), 0.0)
    kb = k * beta
    a_low = jnp.where(strict, _dot3_nt(kb, k, precise) * decay, 0.0)
    eg = jnp.exp(gc_col)
    rhs = jnp.concatenate([v * beta, kb * eg], axis=1)
    tinv = jnp.where(eye, 1.0, 0.0) - a_low
    p = _dot3(a_low, a_low, precise)
    tinv = tinv + _dot3(tinv, p, precise)
    for _ in range(int(math.log2(cc)) - 2):
        p = _dot3(p, p, precise)
        tinv = tinv + _dot3(tinv, p, precise)
    uw = _dot3(tinv, rhs, precise)
    u, wk = uw[:, :DV_A], uw[:, DV_A:]
    qk = jnp.where(incl, _mm_nt(q, k, precise) * decay, 0.0)
    qg = q * eg
    gc_last = jnp.concatenate(
        [jnp.broadcast_to(gc_col[h * cc + cc - 1:h * cc + cc, :], (cc, 1)) for h in range(H_A)], axis=0)
    kd = k * jnp.exp(gc_last - gc_col)
    v_new = []
    o_state = []
    for h in range(H_A):
        sl = slice(h * cc, (h + 1) * cc)
        s_h = s_ref[h].astype(_act_dtype(precise))
        v_new.append(u[sl] - _mm(wk[sl], s_h, precise))
        o_state.append(_mm(qg[sl], s_h, precise))
    v_new = jnp.concatenate(v_new, axis=0)
    o = jnp.concatenate(o_state, axis=0) + _mm(qk, v_new, precise)
    for h in range(H_A):
        sl = slice(h * cc, (h + 1) * cc)
        gl = jnp.exp(gc_col[h * cc + cc - 1:h * cc + cc, :])
        s_ref[h] = s_ref[h] * gl + _mm_tn(kd[sl], v_new[sl], precise)

    @pl.when(c == pl.num_programs(1) - 1)
    def _():
        so_ref[...] = s_ref[...]

    on = o * lax.rsqrt(jnp.mean(o * o, axis=-1, keepdims=True) + EPS) * onorm_ref[...]
    z = z_ref[...]
    for h in range(H_A):
        o_ref[:, h * DV_A:(h + 1) * DV_A] = on[h * cc:(h + 1) * cc] * _silu(z[:, h * DV_A:(h + 1) * DV_A])


def gdn(proj, conv0, s0, cw, cb, alog, dtb, onorm, *, chunk, valid_len, precise=False):
    b, t, _ = proj.shape
    nc = t // chunk
    kern = functools.partial(_gdn_kernel, chunk=chunk, valid_len=valid_len, precise=precise)
    return pl.pallas_call(
        kern,
        grid=(b, nc),
        in_specs=[pl.BlockSpec((None, chunk, QKV_A), lambda i, c: (i, c, 0)),
                  pl.BlockSpec((None, chunk, H_A * DV_A), lambda i, c: (i, c, OFF_Z // (H_A * DV_A))),
                  pl.BlockSpec((None, chunk, LANE), lambda i, c: (i, c, OFF_SMALL // LANE)),
                  pl.BlockSpec((None, SUBLANE, QKV_A), lambda i, c: (i, 0, 0)),
                  pl.BlockSpec((None, H_A, DK_A, DV_A), lambda i, c: (i, 0, 0, 0)),
                  pl.BlockSpec((CONV_A, QKV_A), lambda i, c: (0, 0)),
                  pl.BlockSpec((1, QKV_A), lambda i, c: (0, 0)),
                  pl.BlockSpec((1, H_A), lambda i, c: (0, 0)),
                  pl.BlockSpec((1, H_A), lambda i, c: (0, 0)),
                  pl.BlockSpec((1, DV_A), lambda i, c: (0, 0))],
        out_specs=[pl.BlockSpec((None, chunk, H_A * DV_A), lambda i, c: (i, c, 0)),
                   pl.BlockSpec((None, SUBLANE, QKV_A), lambda i, c: (i, 0, 0)),
                   pl.BlockSpec((None, H_A, DK_A, DV_A), lambda i, c: (i, 0, 0, 0))],
        out_shape=[jax.ShapeDtypeStruct((b, t, H_A * DV_A), F32),
                   jax.ShapeDtypeStruct((b, SUBLANE, QKV_A), F32),
                   jax.ShapeDtypeStruct((b, H_A, DK_A, DV_A), F32)],
        scratch_shapes=[pltpu.VMEM((chunk + SUBLANE, QKV_A), F32),
                        pltpu.VMEM((H_A, DK_A, DV_A), F32)],
        compiler_params=_cparams("parallel", "arbitrary"),
        name="gdn",
    )(proj, proj, proj, conv0, s0, cw, cb.reshape(1, -1), alog.reshape(1, -1), dtb.reshape(1, -1),
      onorm.reshape(1, -1))


def _order_key(score):
    score = jnp.where(score == 0.0, 0.0, score)
    bits = pltpu.bitcast(score, I32)
    return jnp.where(bits < 0, bits ^ jnp.int32(0x7FFFFFFF), bits)


def _kth_largest_key(key, n_sel):
    rows = key.shape[0]

    def count_ge(cand):
        return jnp.sum(jnp.where(key >= cand, 1.0, 0.0), axis=1, keepdims=True)

    int_min = jnp.int32(-2 ** 31)
    zero = jnp.zeros((rows, 1), I32)
    ans = jnp.where(count_ge(zero) >= n_sel, zero, jnp.full((rows, 1), int_min, I32))

    def body(i, ans):
        cand = ans | (jnp.int32(1) << (jnp.int32(30) - i))
        return jnp.where(count_ge(cand) >= n_sel, cand, ans)

    return lax.fori_loop(0, 31, body, ans)


def _select_bias(key, thr, adm, n_sel, ustrict_ref):
    rows, n = key.shape
    need = n_sel - jnp.sum(jnp.where(key > thr, 1.0, 0.0), axis=1, keepdims=True)
    base = jnp.zeros((rows, 1), F32)
    pieces = []
    for c0 in range(0, n, LANE):
        e = jnp.where(key[:, c0:c0 + LANE] == thr, 1.0, 0.0)
        pieces.append(_dot(e.astype(BF16), ustrict_ref[...]) + base)
        base = base + jnp.sum(e, axis=1, keepdims=True)
    rank = jnp.concatenate(pieces, axis=1)
    sel = jnp.where(key > thr, 1.0, jnp.where((key == thr) & (rank < need), 1.0, 0.0))
    return jnp.where(adm & (sel > 0.5), 0.0, NEG)


def _strict_lower_ones(n):
    r = jnp.arange(n)
    return (r[:, None] < r[None, :]).astype(BF16)


def _strict_upper_ones(n):
    r = jnp.arange(n)
    return (r[:, None] > r[None, :]).astype(BF16)


def _dsa_kernel(qi_ref, sm_ref, kit_ref, q_ref, kt_ref, v_ref, us_ref, o_ref, *, tq, n_keys, n_sel, first_block):
    i = pl.program_id(1) + first_block
    q_pos = i * tq + lax.broadcasted_iota(I32, (tq, 1), 0)
    col = lax.broadcasted_iota(I32, (tq, n_keys), 1)
    adm = col <= q_pos
    qi = qi_ref[...].astype(BF16)
    sm = sm_ref[...]
    kit = kit_ref[...]
    score = jnp.zeros((tq, n_keys), F32)
    for h in range(H_IDX):
        rel = jnp.maximum(_dot(qi[:, h * D_IDX:(h + 1) * D_IDX], kit) * (D_IDX ** -0.5), 0.0)
        score = score + rel * (sm[:, 2 * H_A + h:2 * H_A + h + 1] * (H_IDX ** -0.5))
    score = jnp.where(adm, score, NEG)
    key = _order_key(score)
    thr = _kth_largest_key(key, n_sel)
    bias = _select_bias(key, thr, adm, n_sel, us_ref)
    q = q_ref[...]
    v = v_ref[...]
    grp = H_B // HKV_B
    for h in range(H_B):
        g = h // grp
        logits = _dot(q[:, h * DH_B:(h + 1) * DH_B], kt_ref[g * DH_B:(g + 1) * DH_B, :]) * (DH_B ** -0.5) + bias
        m = jnp.max(logits, axis=-1, keepdims=True)
        p = jnp.exp(logits - m)
        s = jnp.sum(p, axis=-1, keepdims=True)
        ov = _dot(p.astype(BF16), v)
        o_ref[:, h * DH_B:(h + 1) * DH_B] = ov[:, g * DH_B:(g + 1) * DH_B] / s


def dsa_prompt(proj, qn, kt, v, kit, *, tq, group):
    b, t, _ = proj.shape
    n_sel = min(TOPK_B, t // 4)
    wq = H_IDX * D_IDX
    us = _strict_lower_ones(LANE)
    outs = []
    for g0 in range(0, t // tq, group):
        n_keys = (g0 + group) * tq
        kern = functools.partial(_dsa_kernel, tq=tq, n_keys=n_keys, n_sel=n_sel, first_block=g0)
        outs.append(pl.pallas_call(
            kern,
            grid=(b, group),
            in_specs=[pl.BlockSpec((None, tq, wq), lambda i, j, g0=g0: (i, j + g0, OFF_QI // wq)),
                      pl.BlockSpec((None, tq, LANE), lambda i, j, g0=g0: (i, j + g0, OFF_SMALL // LANE)),
                      pl.BlockSpec((None, D_IDX, n_keys), lambda i, j: (i, 0, 0)),
                      pl.BlockSpec((None, tq, H_B * DH_B), lambda i, j, g0=g0: (i, j + g0, 0)),
                      pl.BlockSpec((None, HKV_B * DH_B, n_keys), lambda i, j: (i, 0, 0)),
                      pl.BlockSpec((None, n_keys, HKV_B * DH_B), lambda i, j: (i, 0, 0)),
                      pl.BlockSpec((LANE, LANE), lambda i, j: (0, 0))],
            out_specs=pl.BlockSpec((None, tq, H_B * DH_B), lambda i, j: (i, j, 0)),
            out_shape=jax.ShapeDtypeStruct((b, group * tq, H_B * DH_B), F32),
            compiler_params=_cparams("parallel", "parallel"),
            name="dsa_prompt",
        )(proj, proj, kit, qn, kt, v, us))
    return jnp.concatenate(outs, axis=1)


def _dsa_dec_kernel(pt_ref, qi_ref, w_ref, q_ref, knew_ref, vnew_ref, kinew_ref, us_ref, *rest,
                    npp, n_steps, n_past, n_sel, t_new):
    kp = rest[0:npp]
    vp = rest[npp:2 * npp]
    kip = rest[2 * npp:3 * npp]
    o_ref, kt_all, vt_all, kit_all = rest[3 * npp:]
    s = pl.program_id(1)
    page = knew_ref.shape[1]
    for r in range(npp):
        off = pl.multiple_of((s * npp + r) * page, page)
        kt_all[:, pl.ds(off, page)] = kp[r][...]
        vt_all[:, pl.ds(off, page)] = vp[r][...]
        kit_all[:, pl.ds(off, page)] = kip[r][...]

    @pl.when(s == n_steps - 1)
    def _():
        n_keys = n_past + page
        kt_all[:, n_past:n_keys] = knew_ref[...]
        vt_all[:, n_past:n_keys] = vnew_ref[...]
        kit_all[:, n_past:n_keys] = kinew_ref[...]
        rows = SUBLANE
        col = lax.broadcasted_iota(I32, (rows, n_keys), 1)
        q_pos = n_past + lax.broadcasted_iota(I32, (rows, 1), 0)
        adm = (col <= q_pos) & (col < n_past + t_new)
        w = w_ref[...]
        qi = qi_ref[...].reshape(H_IDX * rows, D_IDX)
        rel = jnp.maximum(_dot(qi, kit_all[...], precision=HI) * (D_IDX ** -0.5), 0.0)
        w_rows = jnp.concatenate([w[:, h:h + 1] for h in range(H_IDX)], axis=0) * (H_IDX ** -0.5)
        rel = rel * w_rows
        score = rel[0:rows]
        for h in range(1, H_IDX):
            score = score + rel[h * rows:(h + 1) * rows]
        score = jnp.where(adm, score, NEG)
        key = _order_key(score)
        thr = _kth_largest_key(key, n_sel)
        bias = _select_bias(key, thr, adm, n_sel, us_ref)
        grp = H_B // HKV_B
        bias_g = jnp.concatenate([bias] * grp, axis=0)
        kt = kt_all[...]
        vt = vt_all[...]
        for g in range(HKV_B):
            logits = _dot3(q_ref[g], kt) * (DH_B ** -0.5) + bias_g
            m = jnp.max(logits, axis=-1, keepdims=True)
            p = jnp.exp(logits - m)
            ssum = jnp.sum(p, axis=-1, keepdims=True)
            og = _dot3_nt(p, vt) / ssum
            for hh in range(grp):
                h = g * grp + hh
                o_ref[:, h * DH_B:(h + 1) * DH_B] = og[hh * rows:(hh + 1) * rows, g * DH_B:(g + 1) * DH_B]


def dsa_decode(page_table, qi, w_idx, q_pad, k_new, v_new, ki_new, pool_k, pool_v, pool_ki, li, *, npp, t_new):
    b, n_pages = page_table.shape
    page = pool_k.shape[3]
    kvw = HKV_B * DH_B
    n_steps = n_pages // npp
    n_past = n_pages * page
    n_sel = min(TOPK_B, (n_past + t_new) // 4)
    rows = SUBLANE * (H_B // HKV_B)
    kern = functools.partial(_dsa_dec_kernel, npp=npp, n_steps=n_steps, n_past=n_past, n_sel=n_sel, t_new=t_new)

    def page_spec(r, width):
        return pl.BlockSpec((None, None, width, page), lambda i, s, pt: (pt[i, s * npp + r], li, 0, 0))

    fixed4 = lambda i, s, pt: (i, 0, 0, 0)
    fixed3 = lambda i, s, pt: (i, 0, 0)
    in_specs = [pl.BlockSpec((None, H_IDX, SUBLANE, D_IDX), fixed4),
                pl.BlockSpec((None, SUBLANE, H_IDX), fixed3),
                pl.BlockSpec((None, HKV_B, rows, kvw), fixed4),
                pl.BlockSpec((None, kvw, page), fixed3),
                pl.BlockSpec((None, kvw, page), fixed3),
                pl.BlockSpec((None, D_IDX, page), fixed3),
                pl.BlockSpec((LANE, LANE), lambda i, s, pt: (0, 0))]
    in_specs += [page_spec(r, kvw) for r in range(npp)] * 2
    in_specs += [page_spec(r, D_IDX) for r in range(npp)]
    grid_spec = pltpu.PrefetchScalarGridSpec(
        num_scalar_prefetch=1,
        grid=(b, n_steps),
        in_specs=in_specs,
        out_specs=pl.BlockSpec((None, SUBLANE, H_B * DH_B), fixed3),
        scratch_shapes=[pltpu.VMEM((kvw, n_past + page), F32),
                        pltpu.VMEM((kvw, n_past + page), F32),
                        pltpu.VMEM((D_IDX, n_past + page), F32)])
    return pl.pallas_call(
        kern,
        grid_spec=grid_spec,
        out_shape=jax.ShapeDtypeStruct((b, SUBLANE, H_B * DH_B), F32),
        compiler_params=_cparams("parallel", "arbitrary"),
        name="dsa_decode",
    )(page_table, qi, w_idx, q_pad, k_new, v_new, ki_new, _strict_lower_ones(LANE),
      *([pool_k] * npp), *([pool_v] * npp), *([pool_ki] * npp))


def _cconv_kernel(glu_ref, conv0_ref, cw_ref, cb_ref, lng_ref, lnb_ref, o_ref, convo_ref, win_ref, *, tt, valid_len):
    halo = 4 * SUBLANE

    @pl.when(pl.program_id(1) == 0)
    def _():
        win_ref[0:halo, :] = conv0_ref[...]

    glu = glu_ref[...]
    win_ref[halo:halo + tt, :] = glu[:, :C_W] * _sigmoid(glu[:, C_W:])
    acc = jnp.broadcast_to(cb_ref[...], (tt, C_W))
    for j in range(CONV_C):
        acc = acc + cw_ref[j:j + 1, :] * win_ref[pl.ds(halo - (CONV_C - 1) + j, tt), :]
    tail = win_ref[pl.ds(valid_len, halo), :]
    win_ref[0:halo, :] = tail
    convo_ref[...] = tail
    mu = jnp.mean(acc, axis=-1, keepdims=True)
    xc = acc - mu
    yn = xc * lax.rsqrt(jnp.mean(xc * xc, axis=-1, keepdims=True) + EPS) * lng_ref[...] + lnb_ref[...]
    o_ref[...] = _silu(yn)


def conformer_conv(proj, conv0, cw, cb, lng, lnb, *, tt, valid_len):
    b, t, _ = proj.shape
    halo = 4 * SUBLANE
    kern = functools.partial(_cconv_kernel, tt=tt, valid_len=valid_len)
    cwp = jnp.pad(cw, ((0, halo - CONV_C), (0, 0)))
    return pl.pallas_call(
        kern,
        grid=(b, t // tt),
        in_specs=[pl.BlockSpec((None, tt, 2 * C_W), lambda i, j: (i, j, 0)),
                  pl.BlockSpec((None, halo, C_W), lambda i, j: (i, 0, 0)),
                  pl.BlockSpec((halo, C_W), lambda i, j: (0, 0)),
                  pl.BlockSpec((1, C_W), lambda i, j: (0, 0)),
                  pl.BlockSpec((1, C_W), lambda i, j: (0, 0)),
                  pl.BlockSpec((1, C_W), lambda i, j: (0, 0))],
        out_specs=[pl.BlockSpec((None, tt, C_W), lambda i, j: (i, j, 0)),
                   pl.BlockSpec((None, halo, C_W), lambda i, j: (i, 0, 0))],
        out_shape=[jax.ShapeDtypeStruct((b, t, C_W), F32),
                   jax.ShapeDtypeStruct((b, halo, C_W), F32)],
        scratch_shapes=[pltpu.VMEM((tt + halo, C_W), F32)],
        compiler_params=_cparams("parallel", "arbitrary"),
        name="conformer_conv",
    )(proj, conv0, cwp, cb.reshape(1, -1), lng.reshape(1, -1), lnb.reshape(1, -1))


F32_EXP_UNDERFLOW = -104.0


def _weights_alive(run):
    return jnp.max(run) > F32_EXP_UNDERFLOW


def _sb_block(q, kj, vj, u_ref, carry, before, scale, precise=False, kv_transposed=False):
    run, acc = carry
    z = (_mm(q, kj, precise) if kv_transposed else _mm_nt(q, kj, precise)) * scale
    lk = -_softplus(z)
    if before is not None:
        lk = jnp.where(before, lk, 0.0)
    if precise:
        later = _dot(lk, u_ref[...].astype(F32), precision=HI)
    else:
        hi, lo = _split2(lk)
        later = _dot(hi, u_ref[...]) + _dot(lo, u_ref[...])
    w = jnp.exp(z + lk + later + run)
    if before is not None:
        w = jnp.where(before, w, 0.0)
    acc = acc + (_mm_nt(w, vj, precise) if kv_transposed else _mm(w, vj, precise))
    run = run + later[:, 0:1] + lk[:, 0:1]
    return run, acc


def _sb_kernel(q_ref, k_ref, v_ref, u_ref, o_ref, *, tq):
    i = pl.program_id(2)
    q = q_ref[...]
    scale = DH_D ** -0.5
    rowi = lax.broadcasted_iota(I32, (tq, tq), 0)
    coli = lax.broadcasted_iota(I32, (tq, tq), 1)

    def blk(j, carry, before):
        off = pl.multiple_of(j * tq, tq)
        return _sb_block(q, k_ref[pl.ds(off, tq), :], v_ref[pl.ds(off, tq), :], u_ref, carry, before, scale)

    carry = (jnp.zeros((tq, 1), F32), jnp.zeros((tq, DH_D), F32))
    run, acc = blk(i, carry, coli < rowi)

    def more(c):
        return (c[0] >= 0) & _weights_alive(c[1])

    def step(c):
        run, acc = blk(c[0], (c[1], c[2]), None)
        return c[0] - 1, run, acc

    o_ref[...] = lax.while_loop(more, step, (i - 1, run, acc))[2]


def sb_prompt(q, k, v, *, tq):
    b, h, t, dh = q.shape
    kern = functools.partial(_sb_kernel, tq=tq)
    return pl.pallas_call(
        kern,
        grid=(b, h, t // tq),
        in_specs=[pl.BlockSpec((None, None, tq, dh), lambda i, j, l: (i, j, l, 0)),
                  pl.BlockSpec((None, None, t, dh), lambda i, j, l: (i, j, 0, 0)),
                  pl.BlockSpec((None, None, t, dh), lambda i, j, l: (i, j, 0, 0)),
                  pl.BlockSpec((tq, tq), lambda i, j, l: (0, 0))],
        out_specs=pl.BlockSpec((None, None, tq, dh), lambda i, j, l: (i, j, l, 0)),
        out_shape=jax.ShapeDtypeStruct((b, h, t, dh), F32),
        compiler_params=_cparams("parallel", "parallel", "parallel"),
        name="sb_prompt",
    )(q, k, v, _strict_upper_ones(tq))


def _sb_dec_kernel(pt_ref, qb_ref, knew_ref, vnew_ref, u_ref, kpool_ref, vpool_ref, o_ref, kbuf, vbuf, sem,
                   *, n_pages, li):
    b = pl.program_id(0)
    q = qb_ref[...]
    rows, width = q.shape
    page = knew_ref.shape[1]
    scale = DH_D ** -0.5

    def page_copies(p, slot):
        idx = pt_ref[b, p]
        return (pltpu.make_async_copy(kpool_ref.at[idx, li], kbuf.at[slot], sem.at[0, slot]),
                pltpu.make_async_copy(vpool_ref.at[idx, li], vbuf.at[slot], sem.at[1, slot]))

    def start(p, slot):
        for c in page_copies(p, slot):
            c.start()

    def wait(p, slot):
        for c in page_copies(p, slot):
            c.wait()

    def slot_of(p):
        return (n_pages - 1 - p) & 1

    def rows_of(buf, slot):
        return buf[slot].reshape(width, page)

    start(n_pages - 1, 0)
    qidx = lax.broadcasted_iota(I32, (rows, page), 0) >> int(math.log2(H_D))
    coli = lax.broadcasted_iota(I32, (rows, page), 1)
    carry = (jnp.zeros((rows, 1), F32), jnp.zeros((rows, width), F32))
    run, acc = _sb_block(q, knew_ref[...], vnew_ref[...], u_ref, carry, coli < qidx, scale, precise=True,
                         kv_transposed=True)

    def more(c):
        return (c[0] >= 0) & _weights_alive(c[1])

    def step(c):
        p = c[0]
        slot = slot_of(p)
        wait(p, slot)

        @pl.when(p > 0)
        def _():
            start(p - 1, 1 - slot)

        run, acc = _sb_block(q, rows_of(kbuf, slot), rows_of(vbuf, slot), u_ref, (c[1], c[2]), None, scale,
                             precise=True, kv_transposed=True)
        return p - 1, run, acc

    p_end, run, acc = lax.while_loop(more, step, (n_pages - 1, run, acc))

    @pl.when(p_end >= 0)
    def _():
        wait(p_end, slot_of(p_end))

    lane_head = lax.broadcasted_iota(I32, (H_D, width), 1) >> int(math.log2(DH_D))
    own = lane_head == lax.broadcasted_iota(I32, (H_D, width), 0)
    per_q = acc.reshape(rows // H_D, H_D, width)
    o_ref[...] = jnp.sum(jnp.where(own[None], per_q, 0.0), axis=1)


def sb_decode(page_table, qb, k_new, v_new, pool_k, pool_v, li):
    b, n_pages = page_table.shape
    page = pool_k.shape[4]
    rows, width = qb.shape[1], qb.shape[2]
    kern = functools.partial(_sb_dec_kernel, n_pages=n_pages, li=li)
    fixed3 = lambda i, pt: (i, 0, 0)
    grid_spec = pltpu.PrefetchScalarGridSpec(
        num_scalar_prefetch=1,
        grid=(b,),
        in_specs=[pl.BlockSpec((None, rows, width), fixed3),
                  pl.BlockSpec((None, width, page), fixed3),
                  pl.BlockSpec((None, width, page), fixed3),
                  pl.BlockSpec((page, page), lambda i, pt: (0, 0)),
                  pl.BlockSpec(memory_space=pl.ANY),
                  pl.BlockSpec(memory_space=pl.ANY)],
        out_specs=pl.BlockSpec((None, rows // H_D, width), fixed3),
        scratch_shapes=[pltpu.VMEM((2, H_D, DH_D, page), F32), pltpu.VMEM((2, H_D, DH_D, page), F32),
                        pltpu.SemaphoreType.DMA((2, 2))])
    return pl.pallas_call(
        kern,
        grid_spec=grid_spec,
        out_shape=jax.ShapeDtypeStruct((b, rows // H_D, width), F32),
        compiler_params=_cparams("arbitrary"),
        name="sb_decode",
    )(page_table, qb, k_new, v_new, _strict_upper_ones(page), pool_k, pool_v)


def _out_res_kernel(a_ref, b_ref, wa_ref, wb_ref, x_ref, gt_ref, o_ref, *, precise):
    y = _mm(a_ref[...], wa_ref[...], precise) + _mm(b_ref[...], wb_ref[...], precise)
    o_ref[...] = x_ref[...] + gt_ref[...] * y


def out_residual(a, bm, w, li, x, gate, *, tm, precise=False):
    gsz, t, d = x.shape
    ka, kb = a.shape[2], bm.shape[2]
    assert ka == kb
    return pl.pallas_call(
        functools.partial(_out_res_kernel, precise=precise),
        grid=(gsz, t // tm),
        in_specs=[pl.BlockSpec((None, tm, ka), lambda g_, i: (g_, i, 0)),
                  pl.BlockSpec((None, tm, kb), lambda g_, i: (g_, i, 0)),
                  pl.BlockSpec((None, ka, d), lambda g_, i: (li, 0, 0)),
                  pl.BlockSpec((None, kb, d), lambda g_, i: (li, 1, 0)),
                  pl.BlockSpec((None, tm, d), lambda g_, i: (g_, i, 0)),
                  _param_spec(gate.shape[1], tm, d, 2)],
        out_specs=pl.BlockSpec((None, tm, d), lambda g_, i: (g_, i, 0)),
        out_shape=jax.ShapeDtypeStruct((gsz, t, d), F32),
        compiler_params=_cparams("parallel", "parallel"),
        name="out_residual",
    )(a, bm, w, w, x, gate)


def _ffn_kernel(x_ref, g_ref, sh_ref, sc_ref, gt_ref, wg_ref, wu_ref, wd_ref, o_ref, h_ref, acc_ref, *, precise):
    k = pl.program_id(2)

    @pl.when(k == 0)
    def _():
        h_ref[...] = _modulated_norm(x_ref[...], g_ref[...], sh_ref[...], sc_ref[...]).astype(h_ref.dtype)
        acc_ref[...] = jnp.zeros_like(acc_ref)

    h = h_ref[...]
    a = _mm(h, wg_ref[...], precise)
    u = _mm(h, wu_ref[...], precise)
    acc_ref[...] += _mm(_silu(a) * u, wd_ref[...], precise)

    @pl.when(k == pl.num_programs(2) - 1)
    def _():
        o_ref[...] = x_ref[...] + gt_ref[...] * acc_ref[...]


def ffn_residual(x, g, shift, scale, gate, wg, wu, wd, li, *, tm, tf, precise=False):
    gsz, t, d = x.shape
    f = wg.shape[2]
    pspec = _param_spec(shift.shape[1], tm, d, 3)
    return pl.pallas_call(
        functools.partial(_ffn_kernel, precise=precise),
        grid=(gsz, t // tm, f // tf),
        in_specs=[pl.BlockSpec((None, tm, d), lambda g_, i, k: (g_, i, 0)),
                  pl.BlockSpec((1, d), lambda g_, i, k: (0, 0)),
                  pspec, pspec, pspec,
                  pl.BlockSpec((None, d, tf), lambda g_, i, k: (li, 0, k)),
                  pl.BlockSpec((None, d, tf), lambda g_, i, k: (li, 0, k)),
                  pl.BlockSpec((None, tf, d), lambda g_, i, k: (li, k, 0))],
        out_specs=pl.BlockSpec((None, tm, d), lambda g_, i, k: (g_, i, 0)),
        out_shape=jax.ShapeDtypeStruct((gsz, t, d), F32),
        scratch_shapes=[pltpu.VMEM((tm, d), _act_dtype(precise)), pltpu.VMEM((tm, d), F32)],
        compiler_params=_cparams("parallel", "parallel", "arbitrary"),
        name="ffn_residual",
    )(x, g, shift, scale, gate, wg, wu, wd)


def _router_kernel(x_ref, g_ref, sh_ref, sc_ref, wr_ref, h_ref, r_ref):
    h = _modulated_norm(x_ref[...], g_ref[...], sh_ref[...], sc_ref[...])
    h_ref[...] = h
    logits = _dot(h, wr_ref[...], precision=HI)
    lane = lax.broadcasted_iota(I32, logits.shape, 1)
    lanef = lane.astype(F32)
    logits = jnp.where(lane < N_EXP, logits, -jnp.inf)
    v1 = jnp.max(logits, axis=-1, keepdims=True)
    i1 = jnp.min(jnp.where(logits == v1, lanef, float(LANE)), axis=-1, keepdims=True)
    rest = jnp.where(lanef == i1, -jnp.inf, logits)
    v2 = jnp.max(rest, axis=-1, keepdims=True)
    i2 = jnp.min(jnp.where(rest == v2, lanef, float(LANE)), axis=-1, keepdims=True)
    e2 = jnp.exp(v2 - v1)
    p1 = 1.0 / (1.0 + e2)
    p2 = e2 / (1.0 + e2)
    r_ref[...] = jnp.where(lane == 0, i1, jnp.where(lane == 1, i2, jnp.where(lane == 2, p1, jnp.where(lane == 3, p2, 0.0))))


def moe_router(x, g, shift, scale, w_router, li, *, tm):
    gsz, t, d = x.shape
    wr = jnp.pad(w_router[li], ((0, 0), (0, LANE - N_EXP)))
    pspec = _param_spec(shift.shape[1], tm, d, 2)
    return pl.pallas_call(
        _router_kernel,
        grid=(gsz, t // tm),
        in_specs=[pl.BlockSpec((None, tm, d), lambda g_, i: (g_, i, 0)),
                  pl.BlockSpec((1, d), lambda g_, i: (0, 0)),
                  pspec, pspec,
                  pl.BlockSpec((d, LANE), lambda g_, i: (0, 0))],
        out_specs=[pl.BlockSpec((None, tm, d), lambda g_, i: (g_, i, 0)),
                   pl.BlockSpec((None, tm, LANE), lambda g_, i: (g_, i, 0))],
        out_shape=[jax.ShapeDtypeStruct((gsz, t, d), F32),
                   jax.ShapeDtypeStruct((gsz, t, LANE), F32)],
        compiler_params=_cparams("parallel", "parallel"),
        name="moe_router",
    )(x, g, shift, scale, wr)


def _row_copy(src_ref, dst_ref, src_row, dst_row, sem):
    return pltpu.make_async_copy(src_ref.at[pl.ds(src_row, 1), :], dst_ref.at[pl.ds(dst_row, 1), :], sem)


def _gather_kernel(idx_ref, src_ref, o_ref, sem, *, tg):
    base = pl.program_id(0) * tg

    def issue(r, carry):
        _row_copy(src_ref, o_ref, idx_ref[base + r], r, sem).start()
        return carry

    lax.fori_loop(0, tg, issue, 0, unroll=ROW_DMA_UNROLL)

    def drain(r, carry):
        _row_copy(src_ref, o_ref, 0, r, sem).wait()
        return carry

    lax.fori_loop(0, tg, drain, 0, unroll=ROW_DMA_UNROLL)


def gather_rows(src, idx, *, tg):
    p = idx.shape[0]
    d = src.shape[1]
    grid_spec = pltpu.PrefetchScalarGridSpec(
        num_scalar_prefetch=1,
        grid=(p // tg,),
        in_specs=[pl.BlockSpec(memory_space=pl.ANY)],
        out_specs=pl.BlockSpec((tg, d), lambda i, idx_: (i, 0)),
        scratch_shapes=[pltpu.SemaphoreType.DMA(())])
    return pl.pallas_call(
        functools.partial(_gather_kernel, tg=tg),
        grid_spec=grid_spec,
        out_shape=jax.ShapeDtypeStruct((p, d), F32),
        compiler_params=_cparams("arbitrary"),
        name="gather_rows",
    )(idx, src)


def _expert_kernel(te_ref, act_ref, xs_ref, wg_ref, wu_ref, wd_ref, o_ref, h_ref, acc_ref, *, precise):
    i = pl.program_id(0)
    k = pl.program_id(1)
    nk = pl.num_programs(1)
    active = act_ref[i] == 1

    @pl.when(active & (k == 0))
    def _():
        h_ref[...] = xs_ref[...].astype(h_ref.dtype)
        acc_ref[...] = jnp.zeros_like(acc_ref)

    @pl.when(active)
    def _():
        h = h_ref[...]
        a = _mm(h, wg_ref[...], precise)
        u = _mm(h, wu_ref[...], precise)
        acc_ref[...] += _mm(_silu(a) * u, wd_ref[...], precise)

    @pl.when(active & (k == nk - 1))
    def _():
        o_ref[...] = acc_ref[...]

    @pl.when(jnp.logical_not(active) & (k == nk - 1))
    def _():
        o_ref[...] = jnp.zeros_like(o_ref)


def moe_experts(xs, tile_expert, tile_active, wg, wu, wd, li, *, tm, tf, precise=False):
    p, d = xs.shape
    f = wg.shape[3]
    nk = f // tf

    def kk(i, k, act):
        return jnp.where(act[i] == 1, k, nk - 1)

    grid_spec = pltpu.PrefetchScalarGridSpec(
        num_scalar_prefetch=2,
        grid=(p // tm, nk),
        in_specs=[pl.BlockSpec((tm, d), lambda i, k, te, act: (i, 0)),
                  pl.BlockSpec((None, None, d, tf), lambda i, k, te, act: (li, te[i], 0, kk(i, k, act))),
                  pl.BlockSpec((None, None, d, tf), lambda i, k, te, act: (li, te[i], 0, kk(i, k, act))),
                  pl.BlockSpec((None, None, tf, d), lambda i, k, te, act: (li, te[i], kk(i, k, act), 0))],
        out_specs=pl.BlockSpec((tm, d), lambda i, k, te, act: (i, 0)),
        scratch_shapes=[pltpu.VMEM((tm, d), _act_dtype(precise)), pltpu.VMEM((tm, d), F32)])
    return pl.pallas_call(
        functools.partial(_expert_kernel, precise=precise),
        grid_spec=grid_spec,
        out_shape=jax.ShapeDtypeStruct((p, d), F32),
        compiler_params=_cparams("arbitrary", "arbitrary"),
        name="moe_experts",
    )(tile_expert, tile_active, xs, wg, wu, wd)


def _combine_kernel(p0_ref, p1_ref, ys_ref, x_ref, gt_ref, r_ref, o_ref, a_ref, b_ref, sem, *, tc, t):
    base = pl.program_id(0) * t + pl.program_id(1) * tc

    def issue(r, carry):
        _row_copy(ys_ref, a_ref, p0_ref[base + r], r, sem.at[0]).start()
        _row_copy(ys_ref, b_ref, p1_ref[base + r], r, sem.at[1]).start()
        return carry

    lax.fori_loop(0, tc, issue, 0, unroll=ROW_DMA_UNROLL)

    def drain(r, carry):
        _row_copy(ys_ref, a_ref, 0, r, sem.at[0]).wait()
        _row_copy(ys_ref, b_ref, 0, r, sem.at[1]).wait()
        return carry

    lax.fori_loop(0, tc, drain, 0, unroll=ROW_DMA_UNROLL)
    r = r_ref[...]
    o_ref[...] = x_ref[...] + gt_ref[...] * (r[:, 2:3] * a_ref[...] + r[:, 3:4] * b_ref[...])


def moe_combine(ys, pos0, pos1, x, gate, route, *, tc):
    gsz, t, d = x.shape
    if gate.shape[1] == 1:
        gspec = pl.BlockSpec((None, 1, d), lambda g_, i, a, b: (g_, 0, 0))
    else:
        gspec = pl.BlockSpec((None, tc, d), lambda g_, i, a, b: (g_, i, 0))
    grid_spec = pltpu.PrefetchScalarGridSpec(
        num_scalar_prefetch=2,
        grid=(gsz, t // tc),
        in_specs=[pl.BlockSpec(memory_space=pl.ANY),
                  pl.BlockSpec((None, tc, d), lambda g_, i, a, b: (g_, i, 0)),
                  gspec,
                  pl.BlockSpec((None, tc, LANE), lambda g_, i, a, b: (g_, i, 0))],
        out_specs=pl.BlockSpec((None, tc, d), lambda g_, i, a, b: (g_, i, 0)),
        scratch_shapes=[pltpu.VMEM((tc, d), F32), pltpu.VMEM((tc, d), F32), pltpu.SemaphoreType.DMA((2,))])
    return pl.pallas_call(
        functools.partial(_combine_kernel, tc=tc, t=t),
        grid_spec=grid_spec,
        out_shape=jax.ShapeDtypeStruct((gsz, t, d), F32),
        compiler_params=_cparams("arbitrary", "arbitrary"),
        name="moe_combine",
    )(pos0, pos1, ys, x, gate, route)


def _routing_tables(idx, tm):
    m = idx.shape[0]
    e_flat = idx.reshape(-1)
    onehot = (e_flat[:, None] == jnp.arange(N_EXP, dtype=I32)[None, :]).astype(I32)
    rank = jnp.take_along_axis(jnp.cumsum(onehot, axis=0) - onehot, e_flat[:, None], axis=1)[:, 0]
    counts = jnp.sum(onehot, axis=0)
    padded = ((counts + tm - 1) // tm) * tm
    starts = jnp.cumsum(padded) - padded
    pos = starts[e_flat] + rank
    n_rows = ((2 * m + N_EXP * (tm - 1)) // tm) * tm
    n_tiles = n_rows // tm
    row_token = jnp.zeros((n_rows,), I32).at[pos].set(jnp.arange(2 * m, dtype=I32) // 2)
    tile_start = jnp.arange(n_tiles, dtype=I32) * tm
    ends = jnp.cumsum(padded)
    tile_e = jnp.sum((tile_start[:, None] >= ends[None, :]).astype(I32), axis=1)
    total = ends[-1]
    active = (tile_start < total).astype(I32)
    last_e = jnp.max(jnp.where(counts > 0, jnp.arange(N_EXP, dtype=I32), 0))
    tile_e = jnp.where(active == 1, jnp.minimum(tile_e, N_EXP - 1), last_e).astype(I32)
    return row_token, tile_e, active, pos.reshape(m, 2)


def moe_residual(x, g, shift, scale, gate, w_router, wg, wu, wd, li, *, tm_tok, tm, tf, precise=False):
    gsz, t, d = x.shape
    h, r = moe_router(x, g, shift, scale, w_router, li, tm=tm_tok)
    idx = r.reshape(gsz * t, LANE)[:, 0:2].astype(I32)
    row_token, tile_e, active, pos = _routing_tables(idx, tm)
    xs = gather_rows(h.reshape(gsz * t, d), row_token, tg=min(tm, 256))
    ys = moe_experts(xs, tile_e, active, wg, wu, wd, li, tm=tm, tf=tf, precise=precise)
    return moe_combine(ys, pos[:, 0], pos[:, 1], x, gate, r, tc=min(tm_tok, 256))


def _pad_ab_weight(w):
    s = [QKV_A, H_A * DV_A, H_A, H_A, H_B * DH_B, HKV_B * DH_B, HKV_B * DH_B, H_IDX * D_IDX, D_IDX, H_IDX]
    o = [0]
    for n in s:
        o.append(o[-1] + n)
    qkv, z, a, b, qb, kb, vb, qi, ki, wi = (w[:, :, o[j]:o[j + 1]] for j in range(10))
    zeros = lambda n: jnp.zeros(w.shape[:2] + (n,), w.dtype)
    return jnp.concatenate([qkv, z, qb, kb, vb, qi, ki, zeros(LANE - D_IDX), a, b, wi, zeros(LANE - 2 * H_A - H_IDX)], axis=-1)


def _tok_tile(t, want):
    return want if t % want == 0 else t


def _trunk(x, mods, w, past, ab_w_pad, tokens_per_seq):
    depth = w['w_ada'].shape[0]
    gsz, t, d = x.shape
    nseq = gsz * t // tokens_per_seq
    ts = tokens_per_seq
    decode = past is not None
    new = {name: [] for name in ('a_s', 'a_conv', 'b_k', 'b_v', 'b_kidx', 'c_conv', 'd_k', 'd_v')}
    tm = _tok_tile(t, 512)

    def seq(a):
        return a.reshape(nseq, ts, a.shape[-1])

    def tok(a):
        return a.reshape(gsz, t, a.shape[-1])

    for layer in range(depth):
        li = layer // 2
        shift1, scale1, gate1, shift2, scale2, gate2 = mods[layer]
        g_mix = w['norm_mix'][layer].reshape(1, d)
        g_ffn = w['norm_ffn'][layer].reshape(1, d)
        if layer % 2 == 0:
            proj = mod_linear(x, g_mix, shift1, scale1, ab_w_pad, li, tm=tm, tn=AB_COLS // 2, precise=decode)
            qn = head_norm(proj, OFF_QB // 512, 512, w['ab_qnorm'][li], DH_B, tm=tm)
            kn = head_norm(proj, OFF_KV // 128, 128, w['ab_knorm'][li], DH_B, tm=tm)
            projs = seq(proj)
            v_b = projs[:, :, OFF_KV + 128:OFF_KV + 256]
            k_i = projs[:, :, OFF_KI:OFF_KI + D_IDX]
            kns = seq(kn)
            if not decode:
                conv0 = jnp.zeros((nseq, SUBLANE, QKV_A), F32)
                s0 = jnp.zeros((nseq, H_A, DK_A, DV_A), F32)
                o_a, conv_t, s_new = gdn(projs, conv0, s0, w['ab_conv_w'][li], w['ab_conv_b'][li], w['ab_a_log'][li],
                                         w['ab_dt_bias'][li], w['ab_onorm'][li], chunk=CHUNK_A, valid_len=CHUNK_A)
                o_b = dsa_prompt(projs, seq(qn).astype(BF16), jnp.swapaxes(kns, 1, 2).astype(BF16),
                                 v_b.astype(BF16), jnp.swapaxes(k_i, 1, 2).astype(BF16), tq=128,
                                 group=math.gcd(4, ts // 128))
            else:
                cpad = SUBLANE
                conv0 = jnp.pad(past['a_conv'][li], ((0, 0), (SUBLANE - (CONV_A - 1), 0), (0, 0)))
                proj_p = jnp.pad(projs, ((0, 0), (0, cpad - ts), (0, 0)))
                o_a, conv_t, s_new = gdn(proj_p, conv0, past['a_s'][li], w['ab_conv_w'][li], w['ab_conv_b'][li],
                                         w['ab_a_log'][li], w['ab_dt_bias'][li], w['ab_onorm'][li],
                                         chunk=cpad, valid_len=ts, precise=True)
                o_a = o_a[:, :ts]
                o_b = _dsa_decode_wrap(projs, seq(qn), kns, v_b, k_i, past, li, ts)
            new['a_s'].append(s_new)
            new['a_conv'].append(conv_t[:, SUBLANE - (CONV_A - 1):])
            new['b_k'].append(kns.reshape(nseq, ts, HKV_B, DH_B))
            new['b_v'].append(v_b.reshape(nseq, ts, HKV_B, DH_B))
            new['b_kidx'].append(k_i)
            x = out_residual(tok(o_a), tok(o_b), w['ab_w_out'], li, x, gate1, tm=tm, precise=decode)
            x = ffn_residual(x, g_ffn, shift2, scale2, gate2, w['ff_w_gate'], w['ff_w_up'], w['ff_w_down'], li,
                             tm=_tok_tile(t, 1024), tf=256, precise=decode)
        else:
            proj = mod_linear(x, g_mix, shift1, scale1, w['cd_w_in'], li, tm=tm, tn=1280, precise=decode)
            qn = head_norm(proj, 2, 512, w['cd_qnorm'][li], DH_D, tm=tm)
            kn = head_norm(proj, 3, 512, w['cd_knorm'][li], DH_D, tm=tm)
            projs = seq(proj)
            v_d = projs[:, :, 4 * C_W:5 * C_W]
            kns, qns = seq(kn), seq(qn)
            halo = 4 * SUBLANE
            if not decode:
                conv0 = jnp.zeros((nseq, halo, C_W), F32)
                o_c, conv_t = conformer_conv(projs, conv0, w['cd_conv_w'][li], w['cd_conv_b'][li], w['cd_ln_g'][li],
                                             w['cd_ln_b'][li], tt=512, valid_len=512)
                heads = lambda a: a.reshape(nseq, ts, H_D, DH_D).transpose(0, 2, 1, 3).astype(BF16)
                o_d = sb_prompt(heads(qns), heads(kns), heads(v_d), tq=256)
                o_d = o_d.transpose(0, 2, 1, 3).reshape(nseq, ts, H_D * DH_D)
            else:
                conv0 = jnp.pad(past['c_conv'][li], ((0, 0), (halo - (CONV_C - 1), 0), (0, 0)))
                proj_p = jnp.pad(projs, ((0, 0), (0, SUBLANE - ts), (0, 0)))
                o_c, conv_t = conformer_conv(proj_p, conv0, w['cd_conv_w'][li], w['cd_conv_b'][li], w['cd_ln_g'][li],
                                             w['cd_ln_b'][li], tt=SUBLANE, valid_len=ts)
                o_c = o_c[:, :ts]
                o_d = _sb_decode_wrap(qns, kns, v_d, past, li, ts)
            new['c_conv'].append(conv_t[:, halo - (CONV_C - 1):])
            new['d_k'].append(kns.reshape(nseq, ts, H_D, DH_D))
            new['d_v'].append(v_d.reshape(nseq, ts, H_D, DH_D))
            x = out_residual(tok(o_c), tok(o_d), w['cd_w_out'], li, x, gate1, tm=tm, precise=decode)
            x = moe_residual(x, g_ffn, shift2, scale2, gate2, w['moe_router'], w['moe_w_gate'], w['moe_w_up'],
                             w['moe_w_down'], li, tm_tok=tm, tm=_tok_tile(t, 1024), tf=512, precise=decode)
    stacked = (jnp.stack(new['a_s'], 0), jnp.stack(new['a_conv'], 0),
               jnp.stack(new['b_k'], 2), jnp.stack(new['b_v'], 2), jnp.stack(new['b_kidx'], 2),
               jnp.stack(new['c_conv'], 0), jnp.stack(new['d_k'], 2), jnp.stack(new['d_v'], 2))
    return x, stacked


def _dsa_decode_wrap(projs, qns, kns, v_b, k_i, past, li, ts):
    nseq = projs.shape[0]
    page = past['b_k'].shape[1]
    n_pool, n_layers = past['b_k'].shape[0], past['b_k'].shape[2]
    grp = H_B // HKV_B
    kvw = HKV_B * DH_B
    qi = projs[:, :, OFF_QI:OFF_QI + H_IDX * D_IDX].reshape(nseq, ts, H_IDX, D_IDX).transpose(0, 2, 1, 3)
    qi = jnp.pad(qi, ((0, 0), (0, 0), (0, SUBLANE - ts), (0, 0)))
    w_idx = jnp.pad(projs[:, :, OFF_SMALL + 2 * H_A:OFF_SMALL + 2 * H_A + H_IDX], ((0, 0), (0, SUBLANE - ts), (0, 0)))
    q = qns.reshape(nseq, ts, HKV_B, grp, DH_B).transpose(0, 2, 3, 1, 4)
    q = jnp.pad(q, ((0, 0), (0, 0), (0, 0), (0, SUBLANE - ts), (0, 0)))
    q = q.reshape(nseq, HKV_B, grp * SUBLANE, DH_B)
    q = jnp.stack([jnp.pad(q[:, g], ((0, 0), (0, 0), (g * DH_B, kvw - (g + 1) * DH_B))) for g in range(HKV_B)], 1)
    new_t = lambda a: jnp.swapaxes(jnp.pad(a, ((0, 0), (0, page - ts), (0, 0))), 1, 2)
    pool_k = jnp.transpose(past['b_k'], (0, 2, 3, 4, 1)).reshape(n_pool, n_layers, kvw, page)
    pool_v = jnp.transpose(past['b_v'], (0, 2, 3, 4, 1)).reshape(n_pool, n_layers, kvw, page)
    pool_ki = jnp.transpose(past['b_kidx'], (0, 2, 3, 1))
    o = dsa_decode(past['page_table'], qi, w_idx, q, new_t(kns), new_t(v_b), new_t(k_i), pool_k, pool_v, pool_ki, li,
                   npp=math.gcd(PAGES_PER_STEP, past['page_table'].shape[1]), t_new=ts)
    return o[:, :ts]


def _sb_decode_wrap(qns, kns, v_d, past, li, ts):
    nseq = qns.shape[0]
    page = past['d_k'].shape[1]
    n_layers = past['d_k'].shape[2]
    width = H_D * DH_D
    q = qns.reshape(nseq, ts, H_D, DH_D)
    eye = jnp.eye(H_D, dtype=F32)
    qb = (q[:, :, :, None, :] * eye[None, None, :, :, None]).reshape(nseq, ts * H_D, width)
    new_t = lambda a: jnp.swapaxes(jnp.pad(a, ((0, 0), (0, page - ts), (0, 0))), 1, 2)
    pool_t = lambda a: jnp.transpose(a, (0, 2, 3, 4, 1))
    return sb_decode(past['page_table'], qb, new_t(kns), new_t(v_d), pool_t(past['d_k']), pool_t(past['d_v']), li)


def kernel(x_prompt, x_sample, state_a_s, state_a_conv, cache_b_k, cache_b_v, cache_b_kidx, state_c_conv, cache_d_k, cache_d_v, page_table, c_prompt, c_sample, w_ada, b_ada, norm_mix, norm_ffn, ab_w_in, ab_conv_w, ab_conv_b, ab_a_log, ab_dt_bias, ab_onorm, ab_qnorm, ab_knorm, ab_w_out, cd_w_in, cd_conv_w, cd_conv_b, cd_ln_g, cd_ln_b, cd_qnorm, cd_knorm, cd_w_out, ff_w_gate, ff_w_up, ff_w_down, moe_router, moe_w_gate, moe_w_up, moe_w_down):
    w = {'w_ada': w_ada, 'b_ada': b_ada, 'norm_mix': norm_mix, 'norm_ffn': norm_ffn,
         'ab_conv_w': ab_conv_w, 'ab_conv_b': ab_conv_b, 'ab_a_log': ab_a_log,
         'ab_dt_bias': ab_dt_bias, 'ab_onorm': ab_onorm, 'ab_qnorm': ab_qnorm, 'ab_knorm': ab_knorm,
         'ab_w_out': ab_w_out, 'cd_w_in': cd_w_in, 'cd_conv_w': cd_conv_w, 'cd_conv_b': cd_conv_b,
         'cd_ln_g': cd_ln_g, 'cd_ln_b': cd_ln_b, 'cd_qnorm': cd_qnorm, 'cd_knorm': cd_knorm,
         'cd_w_out': cd_w_out, 'ff_w_gate': ff_w_gate, 'ff_w_up': ff_w_up, 'ff_w_down': ff_w_down,
         'moe_router': moe_router, 'moe_w_gate': moe_w_gate, 'moe_w_up': moe_w_up, 'moe_w_down': moe_w_down}
    past = {'a_s': state_a_s, 'a_conv': state_a_conv, 'b_k': cache_b_k, 'b_v': cache_b_v,
            'b_kidx': cache_b_kidx, 'c_conv': state_c_conv, 'd_k': cache_d_k, 'd_v': cache_d_v,
            'page_table': page_table}
    bp, tp, d = x_prompt.shape
    bs, tsq, _ = x_sample.shape
    depth = w_ada.shape[0]
    rows = bp + bs
    rpad = -rows % SUBLANE
    c_all = jnp.pad(jnp.concatenate([c_prompt, c_sample], axis=0), ((0, rpad), (0, 0)))
    mod = ada_ln(c_all, w_ada, b_ada)
    mod = mod.reshape(depth, rows + rpad, 6, d)
    mods_p = [[mod[l, :bp, j].reshape(bp, 1, d) for j in range(6)] for l in range(depth)]
    mods_s = [[jnp.repeat(mod[l, bp:bp + bs, j], tsq, axis=0).reshape(1, bs * tsq, d) for j in range(6)]
              for l in range(depth)]
    ab_w_pad = _pad_ab_weight(ab_w_in)
    y_p, (pa_s, pa_conv, pb_k, pb_v, pb_kidx, pc_conv, pd_k, pd_v) = _trunk(x_prompt, mods_p, w, None, ab_w_pad, tp)
    y_s, (sa_s, sa_conv, sb_k, sb_v, sb_kidx, sc_conv, sd_k, sd_v) = _trunk(
        x_sample.reshape(1, bs * tsq, d), mods_s, w, past, ab_w_pad, tsq)
    return (y_p, y_s.reshape(bs, tsq, d), pa_s, sa_s, pa_conv, sa_conv, pb_k, sb_k, pb_v, sb_v, pb_kidx, sb_kidx,
            pc_conv, sc_conv, pd_k, sd_k, pd_v, sd_v)
```

```python
import functools
import math

import jax
import jax.numpy as jnp
from jax import lax
from jax.experimental import pallas as pl
from jax.experimental.pallas import tpu as pltpu

F32 = jnp.float32
BF16 = jnp.bfloat16
I32 = jnp.int32
HI = lax.Precision.HIGHEST

EPS = 1e-6
H_A, DK_A, DV_A, CONV_A, CHUNK_A = 4, 128, 128, 4, 64
QKV_A = H_A * (2 * DK_A + DV_A)
H_B, HKV_B, DH_B, H_IDX, D_IDX, TOPK_B = 8, 2, 64, 4, 64, 256
C_W, CONV_C = 512, 31
H_D, DH_D = 8, 64
N_EXP, TOP_E = 8, 2
NEG = -1e30
PAGES_PER_STEP = 16
ROW_DMA_UNROLL = 8
GDN_SEQS_PER_STEP = 2

SUBLANE, LANE = 8, 128
VMEM_LIMIT = 56 * 1024 * 1024

AB_COLS = 3328
OFF_QKV, OFF_Z, OFF_QB, OFF_KV, OFF_QI, OFF_KI, OFF_SMALL = 0, 1536, 2048, 2560, 2816, 3072, 3200


def _cparams(*sem):
    return pltpu.CompilerParams(dimension_semantics=sem, vmem_limit_bytes=VMEM_LIMIT)


def _sigmoid(x):
    return 1.0 / (1.0 + jnp.exp(-x))


def _silu(x):
    return x * _sigmoid(x)


def _softplus(x):
    return jnp.maximum(x, 0.0) + jnp.log(1.0 + jnp.exp(-jnp.abs(x)))


def _dot(a, b, **kw):
    return jnp.dot(a, b, preferred_element_type=F32, **kw)


def _dot_nt(a, b, **kw):
    return lax.dot_general(a, b, (((1,), (1,)), ((), ())), preferred_element_type=F32, **kw)


def _dot_tn(a, b, **kw):
    return lax.dot_general(a, b, (((0,), (0,)), ((), ())), preferred_element_type=F32, **kw)


def _split2(a):
    hi = a.astype(BF16)
    lo = (a - hi.astype(F32)).astype(BF16)
    return hi, lo


def _dot3(a, b, precise=False):
    if precise:
        return _dot(a, b, precision=HI)
    ah, al = _split2(a)
    bh, bl = _split2(b)
    return _dot(ah, bh) + (_dot(ah, bl) + _dot(al, bh))


def _dot3_nt(a, b, precise=False):
    if precise:
        return _dot_nt(a, b, precision=HI)
    ah, al = _split2(a)
    bh, bl = _split2(b)
    return _dot_nt(ah, bh) + (_dot_nt(ah, bl) + _dot_nt(al, bh))


def _mm(a, b, precise):
    if precise:
        return _dot(a.astype(F32), b.astype(F32), precision=HI)
    return _dot(a.astype(BF16), b.astype(BF16))


def _mm_nt(a, b, precise):
    if precise:
        return _dot_nt(a.astype(F32), b.astype(F32), precision=HI)
    return _dot_nt(a.astype(BF16), b.astype(BF16))


def _mm_tn(a, b, precise):
    if precise:
        return _dot_tn(a.astype(F32), b.astype(F32), precision=HI)
    return _dot_tn(a.astype(BF16), b.astype(BF16))


def _act_dtype(precise):
    return F32 if precise else BF16


def _modulated_norm(x, g, shift, scale):
    ms = jnp.mean(x * x, axis=-1, keepdims=True)
    return (x * lax.rsqrt(ms + EPS) * g) * (1.0 + scale) + shift


def _param_spec(r, tm, d, ngrid):
    if r == 1:
        if ngrid == 3:
            return pl.BlockSpec((None, 1, d), lambda g, i, j: (g, 0, 0))
        return pl.BlockSpec((None, 1, d), lambda g, i: (g, 0, 0))
    if ngrid == 3:
        return pl.BlockSpec((None, tm, d), lambda g, i, j: (g, i, 0))
    return pl.BlockSpec((None, tm, d), lambda g, i: (g, i, 0))


def _ada_kernel(c_ref, w_ref, b_ref, o_ref):
    a = _silu(c_ref[...])
    o_ref[...] = _dot(a, w_ref[...], precision=HI) + b_ref[...]


def ada_ln(c, w_ada, b_ada, tn=1536):
    r, d = c.shape
    depth, _, n = w_ada.shape
    return pl.pallas_call(
        _ada_kernel,
        grid=(depth, n // tn),
        in_specs=[pl.BlockSpec((r, d), lambda l, j: (0, 0)),
                  pl.BlockSpec((None, d, tn), lambda l, j: (l, 0, j)),
                  pl.BlockSpec((None, 1, tn), lambda l, j: (l, 0, j))],
        out_specs=pl.BlockSpec((None, r, tn), lambda l, j: (l, 0, j)),
        out_shape=jax.ShapeDtypeStruct((depth, r, n), F32),
        compiler_params=_cparams("parallel", "parallel"),
        name="ada_ln",
    )(c, w_ada, b_ada.reshape(depth, 1, n))


def _mod_linear_kernel(x_ref, g_ref, sh_ref, sc_ref, w_ref, o_ref, h_ref, *, precise):
    @pl.when(pl.program_id(2) == 0)
    def _():
        h_ref[...] = _modulated_norm(x_ref[...], g_ref[...], sh_ref[...], sc_ref[...]).astype(h_ref.dtype)

    o_ref[...] = _mm(h_ref[...], w_ref[...], precise)


def mod_linear(x, g, shift, scale, w, li, *, tm, tn, precise=False):
    gsz, t, d = x.shape
    n = w.shape[2]
    pspec = _param_spec(shift.shape[1], tm, d, 3)
    return pl.pallas_call(
        functools.partial(_mod_linear_kernel, precise=precise),
        grid=(gsz, t // tm, n // tn),
        in_specs=[pl.BlockSpec((None, tm, d), lambda g_, i, j: (g_, i, 0)),
                  pl.BlockSpec((1, d), lambda g_, i, j: (0, 0)),
                  pspec, pspec,
                  pl.BlockSpec((None, d, tn), lambda g_, i, j: (li, 0, j))],
        out_specs=pl.BlockSpec((None, tm, tn), lambda g_, i, j: (g_, i, j)),
        out_shape=jax.ShapeDtypeStruct((gsz, t, n), F32),
        scratch_shapes=[pltpu.VMEM((tm, d), _act_dtype(precise))],
        compiler_params=_cparams("parallel", "parallel", "arbitrary"),
        name="mod_linear",
    )(x, g, shift, scale, w)


def _head_norm_kernel(x_ref, g_ref, bd_ref, o_ref, *, precise):
    x = x_ref[...]
    if precise:
        ms = _dot(x * x, bd_ref[...], precision=HI)
    else:
        hi, lo = _split2(x * x)
        bd = bd_ref[...].astype(BF16)
        ms = _dot(hi, bd) + _dot(lo, bd)
    o_ref[...] = x * lax.rsqrt(ms + EPS) * g_ref[...]


def head_norm(x, col_block, width, g, dh, *, tm, precise=False):
    gsz, t, _ = x.shape
    idx = jnp.arange(width) // dh
    bd = (idx[:, None] == idx[None, :]).astype(F32) / dh
    gt = jnp.tile(g.astype(F32), width // dh).reshape(1, width)
    return pl.pallas_call(
        functools.partial(_head_norm_kernel, precise=precise),
        grid=(gsz, t // tm),
        in_specs=[pl.BlockSpec((None, tm, width), lambda g_, i: (g_, i, col_block)),
                  pl.BlockSpec((1, width), lambda g_, i: (0, 0)),
                  pl.BlockSpec((width, width), lambda g_, i: (0, 0))],
        out_specs=pl.BlockSpec((None, tm, width), lambda g_, i: (g_, i, 0)),
        out_shape=jax.ShapeDtypeStruct((gsz, t, width), F32),
        compiler_params=_cparams("parallel", "parallel"),
        name="head_norm",
    )(x, gt, bd)


def _gdn_kernel(qkv_ref, z_ref, ab_ref, conv0_ref, s0_ref, cw_ref, cb_ref, alog_ref, dtb_ref, onorm_ref,
                o_ref, convo_ref, so_ref, win_ref, s_ref, *, nb, chunk, valid_len, precise):
    c = pl.program_id(1)

    @pl.when(c == 0)
    def _():
        for bb in range(nb):
            win_ref[bb, 0:SUBLANE, :] = conv0_ref[bb]
        s_ref[...] = s0_ref[...]

    for bb in range(nb):
        _gdn_chunk(qkv_ref.at[bb], z_ref.at[bb], ab_ref.at[bb], cw_ref, cb_ref, alog_ref, dtb_ref, onorm_ref,
                   o_ref.at[bb], convo_ref.at[bb], win_ref.at[bb], s_ref.at[bb],
                   chunk=chunk, valid_len=valid_len, precise=precise)

    @pl.when(c == pl.num_programs(1) - 1)
    def _():
        so_ref[...] = s_ref[...]


def _gdn_chunk(qkv_ref, z_ref, ab_ref, cw_ref, cb_ref, alog_ref, dtb_ref, onorm_ref,
               o_ref, convo_ref, win_ref, s_ref, *, chunk, valid_len, precise):
    cc = chunk
    hc = H_A * cc
    tail_rows = SUBLANE

    win_ref[tail_rows:tail_rows + cc, :] = qkv_ref[...]
    acc = jnp.broadcast_to(cb_ref[...], (cc, QKV_A))
    for j in range(CONV_A):
        acc = acc + cw_ref[j:j + 1, :] * win_ref[pl.ds(tail_rows - (CONV_A - 1) + j, cc), :]
    y = _silu(acc)
    tail = win_ref[pl.ds(valid_len, tail_rows), :]
    win_ref[0:tail_rows, :] = tail
    convo_ref[...] = tail

    def l2n(a):
        return a * lax.rsqrt(jnp.sum(a * a, axis=-1, keepdims=True) + EPS)

    def stack(off, width):
        return jnp.concatenate([y[:, off + h * width: off + (h + 1) * width] for h in range(H_A)], axis=0)

    q = l2n(stack(0, DK_A)) * (DK_A ** -0.5)
    k = l2n(stack(H_A * DK_A, DK_A))
    v = stack(2 * H_A * DK_A, DV_A)
    ab = ab_ref[...]
    a_raw = jnp.concatenate([ab[:, h:h + 1] for h in range(H_A)], axis=0)
    b_raw = jnp.concatenate([ab[:, H_A + h:H_A + h + 1] for h in range(H_A)], axis=0)
    alog = jnp.concatenate([jnp.broadcast_to(alog_ref[:, h:h + 1], (cc, 1)) for h in range(H_A)], axis=0)
    dtb = jnp.concatenate([jnp.broadcast_to(dtb_ref[:, h:h + 1], (cc, 1)) for h in range(H_A)], axis=0)
    g_col = -jnp.exp(alog) * _softplus(a_raw + dtb)
    beta = _sigmoid(b_raw)
    rowi = lax.broadcasted_iota(I32, (hc, hc), 0)
    coli = lax.broadcasted_iota(I32, (hc, hc), 1)
    sh = int(math.log2(cc))
    if valid_len < cc:
        rv = (lax.broadcasted_iota(I32, (hc, 1), 0) & (cc - 1)) < valid_len
        g_col = jnp.where(rv, g_col, 0.0)
        beta = jnp.where(rv, beta, 0.0)
    same = (rowi >> sh) == (coli >> sh)
    eye = rowi == coli
    incl = same & (rowi >= coli)
    strict = same & (rowi > coli)

    def to_row(col):
        return jnp.sum(jnp.where(eye, col, 0.0), axis=0, keepdims=True)

    gc_col = jnp.sum(jnp.where(incl, to_row(g_col), 0.0), axis=1, keepdims=True)
    gc_row = to_row(gc_col)
    decay = jnp.where(incl, jnp.exp(jnp.where(incl, gc_col - gc_row, 0.0)), 0.0)
    kb = k * beta
    a_low = jnp.where(strict, _dot3_nt(kb, k, precise) * decay, 0.0)
    eg = jnp.exp(gc_col)
    rhs = jnp.concatenate([v * beta, kb * eg], axis=1)
    tinv = jnp.where(eye, 1.0, 0.0) - a_low
    p = _dot3(a_low, a_low, precise)
    tinv = tinv + _dot3(tinv, p, precise)
    for _ in range(int(math.log2(cc)) - 2):
        p = _dot3(p, p, precise)
        tinv = tinv + _dot3(tinv, p, precise)
    uw = _dot3(tinv, rhs, precise)
    u, wk = uw[:, :DV_A], uw[:, DV_A:]
    qk = jnp.where(incl, _mm_nt(q, k, precise) * decay, 0.0)
    qg = q * eg
    gc_last = jnp.concatenate(
        [jnp.broadcast_to(gc_col[h * cc + cc - 1:h * cc + cc, :], (cc, 1)) for h in range(H_A)], axis=0)
    kd = k * jnp.exp(gc_last - gc_col)
    v_new = []
    o_state = []
    for h in range(H_A):
        sl = slice(h * cc, (h + 1) * cc)
        s_h = s_ref[h].astype(_act_dtype(precise))
        v_new.append(u[sl] - _mm(wk[sl], s_h, precise))
        o_state.append(_mm(qg[sl], s_h, precise))
    v_new = jnp.concatenate(v_new, axis=0)
    o = jnp.concatenate(o_state, axis=0) + _mm(qk, v_new, precise)
    for h in range(H_A):
        sl = slice(h * cc, (h + 1) * cc)
        gl = jnp.exp(gc_col[h * cc + cc - 1:h * cc + cc, :])
        s_ref[h] = s_ref[h] * gl + _mm_tn(kd[sl], v_new[sl], precise)
    on = o * lax.rsqrt(jnp.mean(o * o, axis=-1, keepdims=True) + EPS) * onorm_ref[...]
    z = z_ref[...]
    for h in range(H_A):
        o_ref[:, h * DV_A:(h + 1) * DV_A] = on[h * cc:(h + 1) * cc] * _silu(z[:, h * DV_A:(h + 1) * DV_A])


def gdn(proj, conv0, s0, cw, cb, alog, dtb, onorm, *, chunk, valid_len, precise=False):
    b, t, _ = proj.shape
    nc = t // chunk
    nb = math.gcd(b, GDN_SEQS_PER_STEP)
    kern = functools.partial(_gdn_kernel, nb=nb, chunk=chunk, valid_len=valid_len, precise=precise)
    return pl.pallas_call(
        kern,
        grid=(b // nb, nc),
        in_specs=[pl.BlockSpec((nb, chunk, QKV_A), lambda i, c: (i, c, 0)),
                  pl.BlockSpec((nb, chunk, H_A * DV_A), lambda i, c: (i, c, OFF_Z // (H_A * DV_A))),
                  pl.BlockSpec((nb, chunk, LANE), lambda i, c: (i, c, OFF_SMALL // LANE)),
                  pl.BlockSpec((nb, SUBLANE, QKV_A), lambda i, c: (i, 0, 0)),
                  pl.BlockSpec((nb, H_A, DK_A, DV_A), lambda i, c: (i, 0, 0, 0)),
                  pl.BlockSpec((CONV_A, QKV_A), lambda i, c: (0, 0)),
                  pl.BlockSpec((1, QKV_A), lambda i, c: (0, 0)),
                  pl.BlockSpec((1, H_A), lambda i, c: (0, 0)),
                  pl.BlockSpec((1, H_A), lambda i, c: (0, 0)),
                  pl.BlockSpec((1, DV_A), lambda i, c: (0, 0))],
        out_specs=[pl.BlockSpec((nb, chunk, H_A * DV_A), lambda i, c: (i, c, 0)),
                   pl.BlockSpec((nb, SUBLANE, QKV_A), lambda i, c: (i, 0, 0)),
                   pl.BlockSpec((nb, H_A, DK_A, DV_A), lambda i, c: (i, 0, 0, 0))],
        out_shape=[jax.ShapeDtypeStruct((b, t, H_A * DV_A), F32),
                   jax.ShapeDtypeStruct((b, SUBLANE, QKV_A), F32),
                   jax.ShapeDtypeStruct((b, H_A, DK_A, DV_A), F32)],
        scratch_shapes=[pltpu.VMEM((nb, chunk + SUBLANE, QKV_A), F32),
                        pltpu.VMEM((nb, H_A, DK_A, DV_A), F32)],
        compiler_params=_cparams("parallel", "arbitrary"),
        name="gdn",
    )(proj, proj, proj, conv0, s0, cw, cb.reshape(1, -1), alog.reshape(1, -1), dtb.reshape(1, -1),
      onorm.reshape(1, -1))


def _order_key(score):
    score = jnp.where(score == 0.0, 0.0, score)
    bits = pltpu.bitcast(score, I32)
    return jnp.where(bits < 0, bits ^ jnp.int32(0x7FFFFFFF), bits)


def _kth_largest_key(key, n_sel):
    rows = key.shape[0]

    def count_ge(cand):
        return jnp.sum(jnp.where(key >= cand, 1.0, 0.0), axis=1, keepdims=True)

    int_min = jnp.int32(-2 ** 31)
    zero = jnp.zeros((rows, 1), I32)
    ans = jnp.where(count_ge(zero) >= n_sel, zero, jnp.full((rows, 1), int_min, I32))

    def body(i, ans):
        cand = ans | (jnp.int32(1) << (jnp.int32(30) - i))
        return jnp.where(count_ge(cand) >= n_sel, cand, ans)

    return lax.fori_loop(0, 31, body, ans)


def _select_bias(key, thr, adm, n_sel, ustrict_ref):
    rows, n = key.shape
    need = n_sel - jnp.sum(jnp.where(key > thr, 1.0, 0.0), axis=1, keepdims=True)
    base = jnp.zeros((rows, 1), F32)
    pieces = []
    for c0 in range(0, n, LANE):
        e = jnp.where(key[:, c0:c0 + LANE] == thr, 1.0, 0.0)
        pieces.append(_dot(e.astype(BF16), ustrict_ref[...]) + base)
        base = base + jnp.sum(e, axis=1, keepdims=True)
    rank = jnp.concatenate(pieces, axis=1)
    sel = jnp.where(key > thr, 1.0, jnp.where((key == thr) & (rank < need), 1.0, 0.0))
    return jnp.where(adm & (sel > 0.5), 0.0, NEG)


def _strict_lower_ones(n):
    r = jnp.arange(n)
    return (r[:, None] < r[None, :]).astype(BF16)


def _strict_upper_ones(n):
    r = jnp.arange(n)
    return (r[:, None] > r[None, :]).astype(BF16)


def _dsa_kernel(qi_ref, sm_ref, kit_ref, q_ref, kt_ref, v_ref, us_ref, o_ref, *, tq, n_keys, n_sel, first_block):
    i = pl.program_id(1) + first_block
    q_pos = i * tq + lax.broadcasted_iota(I32, (tq, 1), 0)
    col = lax.broadcasted_iota(I32, (tq, n_keys), 1)
    adm = col <= q_pos
    qi = qi_ref[...].astype(BF16)
    sm = sm_ref[...]
    kit = kit_ref[...]
    score = jnp.zeros((tq, n_keys), F32)
    for h in range(H_IDX):
        rel = jnp.maximum(_dot(qi[:, h * D_IDX:(h + 1) * D_IDX], kit) * (D_IDX ** -0.5), 0.0)
        score = score + rel * (sm[:, 2 * H_A + h:2 * H_A + h + 1] * (H_IDX ** -0.5))
    score = jnp.where(adm, score, NEG)
    key = _order_key(score)
    thr = _kth_largest_key(key, n_sel)
    bias = _select_bias(key, thr, adm, n_sel, us_ref)
    q = q_ref[...]
    v = v_ref[...]
    grp = H_B // HKV_B
    for h in range(H_B):
        g = h // grp
        logits = _dot(q[:, h * DH_B:(h + 1) * DH_B], kt_ref[g * DH_B:(g + 1) * DH_B, :]) * (DH_B ** -0.5) + bias
        m = jnp.max(logits, axis=-1, keepdims=True)
        p = jnp.exp(logits - m)
        s = jnp.sum(p, axis=-1, keepdims=True)
        ov = _dot(p.astype(BF16), v)
        o_ref[:, h * DH_B:(h + 1) * DH_B] = ov[:, g * DH_B:(g + 1) * DH_B] / s


def dsa_prompt(proj, qn, kt, v, kit, *, tq, group):
    b, t, _ = proj.shape
    n_sel = min(TOPK_B, t // 4)
    wq = H_IDX * D_IDX
    us = _strict_lower_ones(LANE)
    outs = []
    for g0 in range(0, t // tq, group):
        n_keys = (g0 + group) * tq
        kern = functools.partial(_dsa_kernel, tq=tq, n_keys=n_keys, n_sel=n_sel, first_block=g0)
        outs.append(pl.pallas_call(
            kern,
            grid=(b, group),
            in_specs=[pl.BlockSpec((None, tq, wq), lambda i, j, g0=g0: (i, j + g0, OFF_QI // wq)),
                      pl.BlockSpec((None, tq, LANE), lambda i, j, g0=g0: (i, j + g0, OFF_SMALL // LANE)),
                      pl.BlockSpec((None, D_IDX, n_keys), lambda i, j: (i, 0, 0)),
                      pl.BlockSpec((None, tq, H_B * DH_B), lambda i, j, g0=g0: (i, j + g0, 0)),
                      pl.BlockSpec((None, HKV_B * DH_B, n_keys), lambda i, j: (i, 0, 0)),
                      pl.BlockSpec((None, n_keys, HKV_B * DH_B), lambda i, j: (i, 0, 0)),
                      pl.BlockSpec((LANE, LANE), lambda i, j: (0, 0))],
            out_specs=pl.BlockSpec((None, tq, H_B * DH_B), lambda i, j: (i, j, 0)),
            out_shape=jax.ShapeDtypeStruct((b, group * tq, H_B * DH_B), F32),
            compiler_params=_cparams("parallel", "parallel"),
            name="dsa_prompt",
        )(proj, proj, kit, qn, kt, v, us))
    return jnp.concatenate(outs, axis=1)


def _dsa_dec_kernel(pt_ref, qi_ref, w_ref, q_ref, knew_ref, vnew_ref, kinew_ref, us_ref, *rest,
                    npp, n_steps, n_past, n_sel, t_new):
    kp = rest[0:npp]
    vp = rest[npp:2 * npp]
    kip = rest[2 * npp:3 * npp]
    o_ref, kt_all, vt_all, kit_all = rest[3 * npp:]
    s = pl.program_id(1)
    page = knew_ref.shape[1]
    for r in range(npp):
        off = pl.multiple_of((s * npp + r) * page, page)
        kt_all[:, pl.ds(off, page)] = kp[r][...]
        vt_all[:, pl.ds(off, page)] = vp[r][...]
        kit_all[:, pl.ds(off, page)] = kip[r][...]

    @pl.when(s == n_steps - 1)
    def _():
        n_keys = n_past + page
        kt_all[:, n_past:n_keys] = knew_ref[...]
        vt_all[:, n_past:n_keys] = vnew_ref[...]
        kit_all[:, n_past:n_keys] = kinew_ref[...]
        rows = SUBLANE
        col = lax.broadcasted_iota(I32, (rows, n_keys), 1)
        q_pos = n_past + lax.broadcasted_iota(I32, (rows, 1), 0)
        adm = (col <= q_pos) & (col < n_past + t_new)
        w = w_ref[...]
        qi = qi_ref[...].reshape(H_IDX * rows, D_IDX)
        rel = jnp.maximum(_dot(qi, kit_all[...], precision=HI) * (D_IDX ** -0.5), 0.0)
        w_rows = jnp.concatenate([w[:, h:h + 1] for h in range(H_IDX)], axis=0) * (H_IDX ** -0.5)
        rel = rel * w_rows
        score = rel[0:rows]
        for h in range(1, H_IDX):
            score = score + rel[h * rows:(h + 1) * rows]
        score = jnp.where(adm, score, NEG)
        key = _order_key(score)
        thr = _kth_largest_key(key, n_sel)
        bias = _select_bias(key, thr, adm, n_sel, us_ref)
        grp = H_B // HKV_B
        bias_g = jnp.concatenate([bias] * grp, axis=0)
        kt = kt_all[...]
        vt = vt_all[...]
        for g in range(HKV_B):
            logits = _dot3(q_ref[g], kt) * (DH_B ** -0.5) + bias_g
            m = jnp.max(logits, axis=-1, keepdims=True)
            p = jnp.exp(logits - m)
            ssum = jnp.sum(p, axis=-1, keepdims=True)
            og = _dot3_nt(p, vt) / ssum
            for hh in range(grp):
                h = g * grp + hh
                o_ref[:, h * DH_B:(h + 1) * DH_B] = og[hh * rows:(hh + 1) * rows, g * DH_B:(g + 1) * DH_B]


def dsa_decode(page_table, qi, w_idx, q_pad, k_new, v_new, ki_new, pool_k, pool_v, pool_ki, li, *, npp, t_new):
    b, n_pages = page_table.shape
    page = pool_k.shape[3]
    kvw = HKV_B * DH_B
    n_steps = n_pages // npp
    n_past = n_pages * page
    n_sel = min(TOPK_B, (n_past + t_new) // 4)
    rows = SUBLANE * (H_B // HKV_B)
    kern = functools.partial(_dsa_dec_kernel, npp=npp, n_steps=n_steps, n_past=n_past, n_sel=n_sel, t_new=t_new)

    def page_spec(r, width):
        return pl.BlockSpec((None, None, width, page), lambda i, s, pt: (pt[i, s * npp + r], li, 0, 0))

    fixed4 = lambda i, s, pt: (i, 0, 0, 0)
    fixed3 = lambda i, s, pt: (i, 0, 0)
    in_specs = [pl.BlockSpec((None, H_IDX, SUBLANE, D_IDX), fixed4),
                pl.BlockSpec((None, SUBLANE, H_IDX), fixed3),
                pl.BlockSpec((None, HKV_B, rows, kvw), fixed4),
                pl.BlockSpec((None, kvw, page), fixed3),
                pl.BlockSpec((None, kvw, page), fixed3),
                pl.BlockSpec((None, D_IDX, page), fixed3),
                pl.BlockSpec((LANE, LANE), lambda i, s, pt: (0, 0))]
    in_specs += [page_spec(r, kvw) for r in range(npp)] * 2
    in_specs += [page_spec(r, D_IDX) for r in range(npp)]
    grid_spec = pltpu.PrefetchScalarGridSpec(
        num_scalar_prefetch=1,
        grid=(b, n_steps),
        in_specs=in_specs,
        out_specs=pl.BlockSpec((None, SUBLANE, H_B * DH_B), fixed3),
        scratch_shapes=[pltpu.VMEM((kvw, n_past + page), F32),
                        pltpu.VMEM((kvw, n_past + page), F32),
                        pltpu.VMEM((D_IDX, n_past + page), F32)])
    return pl.pallas_call(
        kern,
        grid_spec=grid_spec,
        out_shape=jax.ShapeDtypeStruct((b, SUBLANE, H_B * DH_B), F32),
        compiler_params=_cparams("parallel", "arbitrary"),
        name="dsa_decode",
    )(page_table, qi, w_idx, q_pad, k_new, v_new, ki_new, _strict_lower_ones(LANE),
      *([pool_k] * npp), *([pool_v] * npp), *([pool_ki] * npp))


def _cconv_kernel(glu_ref, conv0_ref, cw_ref, cb_ref, lng_ref, lnb_ref, o_ref, convo_ref, win_ref, *, tt, valid_len):
    halo = 4 * SUBLANE

    @pl.when(pl.program_id(1) == 0)
    def _():
        win_ref[0:halo, :] = conv0_ref[...]

    glu = glu_ref[...]
    win_ref[halo:halo + tt, :] = glu[:, :C_W] * _sigmoid(glu[:, C_W:])
    acc = jnp.broadcast_to(cb_ref[...], (tt, C_W))
    for j in range(CONV_C):
        acc = acc + cw_ref[j:j + 1, :] * win_ref[pl.ds(halo - (CONV_C - 1) + j, tt), :]
    tail = win_ref[pl.ds(valid_len, halo), :]
    win_ref[0:halo, :] = tail
    convo_ref[...] = tail
    mu = jnp.mean(acc, axis=-1, keepdims=True)
    xc = acc - mu
    yn = xc * lax.rsqrt(jnp.mean(xc * xc, axis=-1, keepdims=True) + EPS) * lng_ref[...] + lnb_ref[...]
    o_ref[...] = _silu(yn)


def conformer_conv(proj, conv0, cw, cb, lng, lnb, *, tt, valid_len):
    b, t, _ = proj.shape
    halo = 4 * SUBLANE
    kern = functools.partial(_cconv_kernel, tt=tt, valid_len=valid_len)
    cwp = jnp.pad(cw, ((0, halo - CONV_C), (0, 0)))
    return pl.pallas_call(
        kern,
        grid=(b, t // tt),
        in_specs=[pl.BlockSpec((None, tt, 2 * C_W), lambda i, j: (i, j, 0)),
                  pl.BlockSpec((None, halo, C_W), lambda i, j: (i, 0, 0)),
                  pl.BlockSpec((halo, C_W), lambda i, j: (0, 0)),
                  pl.BlockSpec((1, C_W), lambda i, j: (0, 0)),
                  pl.BlockSpec((1, C_W), lambda i, j: (0, 0)),
                  pl.BlockSpec((1, C_W), lambda i, j: (0, 0))],
        out_specs=[pl.BlockSpec((None, tt, C_W), lambda i, j: (i, j, 0)),
                   pl.BlockSpec((None, halo, C_W), lambda i, j: (i, 0, 0))],
        out_shape=[jax.ShapeDtypeStruct((b, t, C_W), F32),
                   jax.ShapeDtypeStruct((b, halo, C_W), F32)],
        scratch_shapes=[pltpu.VMEM((tt + halo, C_W), F32)],
        compiler_params=_cparams("parallel", "arbitrary"),
        name="conformer_conv",
    )(proj, conv0, cwp, cb.reshape(1, -1), lng.reshape(1, -1), lnb.reshape(1, -1))


F32_EXP_UNDERFLOW = -104.0


def _weights_alive(run):
    return jnp.max(run) > F32_EXP_UNDERFLOW


def _sb_block(q, kj, vj, u_ref, carry, before, scale, precise=False, kv_transposed=False):
    run, acc = carry
    z = (_mm(q, kj, precise) if kv_transposed else _mm_nt(q, kj, precise)) * scale
    lk = -_softplus(z)
    if before is not None:
        lk = jnp.where(before, lk, 0.0)
    if precise:
        later = _dot(lk, u_ref[...].astype(F32), precision=HI)
    else:
        hi, lo = _split2(lk)
        later = _dot(hi, u_ref[...]) + _dot(lo, u_ref[...])
    w = jnp.exp(z + lk + later + run)
    if before is not None:
        w = jnp.where(before, w, 0.0)
    acc = acc + (_mm_nt(w, vj, precise) if kv_transposed else _mm(w, vj, precise))
    run = run + later[:, 0:1] + lk[:, 0:1]
    return run, acc


def _sb_kernel(q_ref, k_ref, v_ref, u_ref, o_ref, *, tq):
    i = pl.program_id(2)
    q = q_ref[...]
    scale = DH_D ** -0.5
    rowi = lax.broadcasted_iota(I32, (tq, tq), 0)
    coli = lax.broadcasted_iota(I32, (tq, tq), 1)

    def blk(j, carry, before):
        off = pl.multiple_of(j * tq, tq)
        return _sb_block(q, k_ref[pl.ds(off, tq), :], v_ref[pl.ds(off, tq), :], u_ref, carry, before, scale)

    carry = (jnp.zeros((tq, 1), F32), jnp.zeros((tq, DH_D), F32))
    run, acc = blk(i, carry, coli < rowi)

    def more(c):
        return (c[0] >= 0) & _weights_alive(c[1])

    def step(c):
        run, acc = blk(c[0], (c[1], c[2]), None)
        return c[0] - 1, run, acc

    o_ref[...] = lax.while_loop(more, step, (i - 1, run, acc))[2]


def sb_prompt(q, k, v, *, tq):
    b, h, t, dh = q.shape
    kern = functools.partial(_sb_kernel, tq=tq)
    return pl.pallas_call(
        kern,
        grid=(b, h, t // tq),
        in_specs=[pl.BlockSpec((None, None, tq, dh), lambda i, j, l: (i, j, l, 0)),
                  pl.BlockSpec((None, None, t, dh), lambda i, j, l: (i, j, 0, 0)),
                  pl.BlockSpec((None, None, t, dh), lambda i, j, l: (i, j, 0, 0)),
                  pl.BlockSpec((tq, tq), lambda i, j, l: (0, 0))],
        out_specs=pl.BlockSpec((None, None, tq, dh), lambda i, j, l: (i, j, l, 0)),
        out_shape=jax.ShapeDtypeStruct((b, h, t, dh), F32),
        compiler_params=_cparams("parallel", "parallel", "parallel"),
        name="sb_prompt",
    )(q, k, v, _strict_upper_ones(tq))


def _sb_dec_kernel(pt_ref, qb_ref, knew_ref, vnew_ref, u_ref, kpool_ref, vpool_ref, o_ref, kbuf, vbuf, sem,
                   *, n_pages, li):
    b = pl.program_id(0)
    q = qb_ref[...]
    rows, width = q.shape
    page = knew_ref.shape[1]
    scale = DH_D ** -0.5

    def page_copies(p, slot):
        idx = pt_ref[b, p]
        return (pltpu.make_async_copy(kpool_ref.at[idx, li], kbuf.at[slot], sem.at[0, slot]),
                pltpu.make_async_copy(vpool_ref.at[idx, li], vbuf.at[slot], sem.at[1, slot]))

    def start(p, slot):
        for c in page_copies(p, slot):
            c.start()

    def wait(p, slot):
        for c in page_copies(p, slot):
            c.wait()

    def slot_of(p):
        return (n_pages - 1 - p) & 1

    def rows_of(buf, slot):
        return buf[slot].reshape(width, page)

    start(n_pages - 1, 0)
    qidx = lax.broadcasted_iota(I32, (rows, page), 0) >> int(math.log2(H_D))
    coli = lax.broadcasted_iota(I32, (rows, page), 1)
    carry = (jnp.zeros((rows, 1), F32), jnp.zeros((rows, width), F32))
    run, acc = _sb_block(q, knew_ref[...], vnew_ref[...], u_ref, carry, coli < qidx, scale, precise=True,
                         kv_transposed=True)

    def more(c):
        return (c[0] >= 0) & _weights_alive(c[1])

    def step(c):
        p = c[0]
        slot = slot_of(p)
        wait(p, slot)

        @pl.when(p > 0)
        def _():
            start(p - 1, 1 - slot)

        run, acc = _sb_block(q, rows_of(kbuf, slot), rows_of(vbuf, slot), u_ref, (c[1], c[2]), None, scale,
                             precise=True, kv_transposed=True)
        return p - 1, run, acc

    p_end, run, acc = lax.while_loop(more, step, (n_pages - 1, run, acc))

    @pl.when(p_end >= 0)
    def _():
        wait(p_end, slot_of(p_end))

    lane_head = lax.broadcasted_iota(I32, (H_D, width), 1) >> int(math.log2(DH_D))
    own = lane_head == lax.broadcasted_iota(I32, (H_D, width), 0)
    per_q = acc.reshape(rows // H_D, H_D, width)
    o_ref[...] = jnp.sum(jnp.where(own[None], per_q, 0.0), axis=1)


def sb_decode(page_table, qb, k_new, v_new, pool_k, pool_v, li):
    b, n_pages = page_table.shape
    page = pool_k.shape[4]
    rows, width = qb.shape[1], qb.shape[2]
    kern = functools.partial(_sb_dec_kernel, n_pages=n_pages, li=li)
    fixed3 = lambda i, pt: (i, 0, 0)
    grid_spec = pltpu.PrefetchScalarGridSpec(
        num_scalar_prefetch=1,
        grid=(b,),
        in_specs=[pl.BlockSpec((None, rows, width), fixed3),
                  pl.BlockSpec((None, width, page), fixed3),
                  pl.BlockSpec((None, width, page), fixed3),
                  pl.BlockSpec((page, page), lambda i, pt: (0, 0)),
                  pl.BlockSpec(memory_space=pl.ANY),
                  pl.BlockSpec(memory_space=pl.ANY)],
        out_specs=pl.BlockSpec((None, rows // H_D, width), fixed3),
        scratch_shapes=[pltpu.VMEM((2, H_D, DH_D, page), F32), pltpu.VMEM((2, H_D, DH_D, page), F32),
                        pltpu.SemaphoreType.DMA((2, 2))])
    return pl.pallas_call(
        kern,
        grid_spec=grid_spec,
        out_shape=jax.ShapeDtypeStruct((b, rows // H_D, width), F32),
        compiler_params=_cparams("arbitrary"),
        name="sb_decode",
    )(page_table, qb, k_new, v_new, _strict_upper_ones(page), pool_k, pool_v)


def _out_res_kernel(a_ref, b_ref, wa_ref, wb_ref, x_ref, gt_ref, o_ref, *, precise):
    y = _mm(a_ref[...], wa_ref[...], precise) + _mm(b_ref[...], wb_ref[...], precise)
    o_ref[...] = x_ref[...] + gt_ref[...] * y


def out_residual(a, bm, w, li, x, gate, *, tm, precise=False):
    gsz, t, d = x.shape
    ka, kb = a.shape[2], bm.shape[2]
    assert ka == kb
    return pl.pallas_call(
        functools.partial(_out_res_kernel, precise=precise),
        grid=(gsz, t // tm),
        in_specs=[pl.BlockSpec((None, tm, ka), lambda g_, i: (g_, i, 0)),
                  pl.BlockSpec((None, tm, kb), lambda g_, i: (g_, i, 0)),
                  pl.BlockSpec((None, ka, d), lambda g_, i: (li, 0, 0)),
                  pl.BlockSpec((None, kb, d), lambda g_, i: (li, 1, 0)),
                  pl.BlockSpec((None, tm, d), lambda g_, i: (g_, i, 0)),
                  _param_spec(gate.shape[1], tm, d, 2)],
        out_specs=pl.BlockSpec((None, tm, d), lambda g_, i: (g_, i, 0)),
        out_shape=jax.ShapeDtypeStruct((gsz, t, d), F32),
        compiler_params=_cparams("parallel", "parallel"),
        name="out_residual",
    )(a, bm, w, w, x, gate)


def _ffn_kernel(x_ref, g_ref, sh_ref, sc_ref, gt_ref, wg_ref, wu_ref, wd_ref, o_ref, h_ref, acc_ref, *, precise):
    k = pl.program_id(2)

    @pl.when(k == 0)
    def _():
        h_ref[...] = _modulated_norm(x_ref[...], g_ref[...], sh_ref[...], sc_ref[...]).astype(h_ref.dtype)
        acc_ref[...] = jnp.zeros_like(acc_ref)

    h = h_ref[...]
    a = _mm(h, wg_ref[...], precise)
    u = _mm(h, wu_ref[...], precise)
    acc_ref[...] += _mm(_silu(a) * u, wd_ref[...], precise)

    @pl.when(k == pl.num_programs(2) - 1)
    def _():
        o_ref[...] = x_ref[...] + gt_ref[...] * acc_ref[...]


def ffn_residual(x, g, shift, scale, gate, wg, wu, wd, li, *, tm, tf, precise=False):
    gsz, t, d = x.shape
    f = wg.shape[2]
    pspec = _param_spec(shift.shape[1], tm, d, 3)
    return pl.pallas_call(
        functools.partial(_ffn_kernel, precise=precise),
        grid=(gsz, t // tm, f // tf),
        in_specs=[pl.BlockSpec((None, tm, d), lambda g_, i, k: (g_, i, 0)),
                  pl.BlockSpec((1, d), lambda g_, i, k: (0, 0)),
                  pspec, pspec, pspec,
                  pl.BlockSpec((None, d, tf), lambda g_, i, k: (li, 0, k)),
                  pl.BlockSpec((None, d, tf), lambda g_, i, k: (li, 0, k)),
                  pl.BlockSpec((None, tf, d), lambda g_, i, k: (li, k, 0))],
        out_specs=pl.BlockSpec((None, tm, d), lambda g_, i, k: (g_, i, 0)),
        out_shape=jax.ShapeDtypeStruct((gsz, t, d), F32),
        scratch_shapes=[pltpu.VMEM((tm, d), _act_dtype(precise)), pltpu.VMEM((tm, d), F32)],
        compiler_params=_cparams("parallel", "parallel", "arbitrary"),
        name="ffn_residual",
    )(x, g, shift, scale, gate, wg, wu, wd)


def _router_kernel(x_ref, g_ref, sh_ref, sc_ref, wr_ref, h_ref, r_ref):
    h = _modulated_norm(x_ref[...], g_ref[...], sh_ref[...], sc_ref[...])
    h_ref[...] = h
    logits = _dot(h, wr_ref[...], precision=HI)
    lane = lax.broadcasted_iota(I32, logits.shape, 1)
    lanef = lane.astype(F32)
    logits = jnp.where(lane < N_EXP, logits, -jnp.inf)
    v1 = jnp.max(logits, axis=-1, keepdims=True)
    i1 = jnp.min(jnp.where(logits == v1, lanef, float(LANE)), axis=-1, keepdims=True)
    rest = jnp.where(lanef == i1, -jnp.inf, logits)
    v2 = jnp.max(rest, axis=-1, keepdims=True)
    i2 = jnp.min(jnp.where(rest == v2, lanef, float(LANE)), axis=-1, keepdims=True)
    e2 = jnp.exp(v2 - v1)
    p1 = 1.0 / (1.0 + e2)
    p2 = e2 / (1.0 + e2)
    r_ref[...] = jnp.where(lane == 0, i1, jnp.where(lane == 1, i2, jnp.where(lane == 2, p1, jnp.where(lane == 3, p2, 0.0))))


def moe_router(x, g, shift, scale, w_router, li, *, tm):
    gsz, t, d = x.shape
    wr = jnp.pad(w_router[li], ((0, 0), (0, LANE - N_EXP)))
    pspec = _param_spec(shift.shape[1], tm, d, 2)
    return pl.pallas_call(
        _router_kernel,
        grid=(gsz, t // tm),
        in_specs=[pl.BlockSpec((None, tm, d), lambda g_, i: (g_, i, 0)),
                  pl.BlockSpec((1, d), lambda g_, i: (0, 0)),
                  pspec, pspec,
                  pl.BlockSpec((d, LANE), lambda g_, i: (0, 0))],
        out_specs=[pl.BlockSpec((None, tm, d), lambda g_, i: (g_, i, 0)),
                   pl.BlockSpec((None, tm, LANE), lambda g_, i: (g_, i, 0))],
        out_shape=[jax.ShapeDtypeStruct((gsz, t, d), F32),
                   jax.ShapeDtypeStruct((gsz, t, LANE), F32)],
        compiler_params=_cparams("parallel", "parallel"),
        name="moe_router",
    )(x, g, shift, scale, wr)


def _row_copy(src_ref, dst_ref, src_row, dst_row, sem):
    return pltpu.make_async_copy(src_ref.at[pl.ds(src_row, 1), :], dst_ref.at[pl.ds(dst_row, 1), :], sem)


def _gather_kernel(idx_ref, src_ref, o_ref, sem, *, tg):
    base = pl.program_id(0) * tg

    def issue(r, carry):
        _row_copy(src_ref, o_ref, idx_ref[base + r], r, sem).start()
        return carry

    lax.fori_loop(0, tg, issue, 0, unroll=ROW_DMA_UNROLL)

    def drain(r, carry):
        _row_copy(src_ref, o_ref, 0, r, sem).wait()
        return carry

    lax.fori_loop(0, tg, drain, 0, unroll=ROW_DMA_UNROLL)


def gather_rows(src, idx, *, tg):
    p = idx.shape[0]
    d = src.shape[1]
    grid_spec = pltpu.PrefetchScalarGridSpec(
        num_scalar_prefetch=1,
        grid=(p // tg,),
        in_specs=[pl.BlockSpec(memory_space=pl.ANY)],
        out_specs=pl.BlockSpec((tg, d), lambda i, idx_: (i, 0)),
        scratch_shapes=[pltpu.SemaphoreType.DMA(())])
    return pl.pallas_call(
        functools.partial(_gather_kernel, tg=tg),
        grid_spec=grid_spec,
        out_shape=jax.ShapeDtypeStruct((p, d), F32),
        compiler_params=_cparams("arbitrary"),
        name="gather_rows",
    )(idx, src)


def _expert_kernel(te_ref, act_ref, xs_ref, wg_ref, wu_ref, wd_ref, o_ref, h_ref, acc_ref, *, precise):
    i = pl.program_id(0)
    k = pl.program_id(1)
    nk = pl.num_programs(1)
    active = act_ref[i] == 1

    @pl.when(active & (k == 0))
    def _():
        h_ref[...] = xs_ref[...].astype(h_ref.dtype)
        acc_ref[...] = jnp.zeros_like(acc_ref)

    @pl.when(active)
    def _():
        h = h_ref[...]
        a = _mm(h, wg_ref[...], precise)
        u = _mm(h, wu_ref[...], precise)
        acc_ref[...] += _mm(_silu(a) * u, wd_ref[...], precise)

    @pl.when(active & (k == nk - 1))
    def _():
        o_ref[...] = acc_ref[...]

    @pl.when(jnp.logical_not(active) & (k == nk - 1))
    def _():
        o_ref[...] = jnp.zeros_like(o_ref)


def moe_experts(xs, tile_expert, tile_active, wg, wu, wd, li, *, tm, tf, precise=False):
    p, d = xs.shape
    f = wg.shape[3]
    nk = f // tf

    def kk(i, k, act):
        return jnp.where(act[i] == 1, k, nk - 1)

    grid_spec = pltpu.PrefetchScalarGridSpec(
        num_scalar_prefetch=2,
        grid=(p // tm, nk),
        in_specs=[pl.BlockSpec((tm, d), lambda i, k, te, act: (i, 0)),
                  pl.BlockSpec((None, None, d, tf), lambda i, k, te, act: (li, te[i], 0, kk(i, k, act))),
                  pl.BlockSpec((None, None, d, tf), lambda i, k, te, act: (li, te[i], 0, kk(i, k, act))),
                  pl.BlockSpec((None, None, tf, d), lambda i, k, te, act: (li, te[i], kk(i, k, act), 0))],
        out_specs=pl.BlockSpec((tm, d), lambda i, k, te, act: (i, 0)),
        scratch_shapes=[pltpu.VMEM((tm, d), _act_dtype(precise)), pltpu.VMEM((tm, d), F32)])
    return pl.pallas_call(
        functools.partial(_expert_kernel, precise=precise),
        grid_spec=grid_spec,
        out_shape=jax.ShapeDtypeStruct((p, d), F32),
        compiler_params=_cparams("arbitrary", "arbitrary"),
        name="moe_experts",
    )(tile_expert, tile_active, xs, wg, wu, wd)


def _combine_kernel(p0_ref, p1_ref, ys_ref, x_ref, gt_ref, r_ref, o_ref, a_ref, b_ref, sem, *, tc, t):
    base = pl.program_id(0) * t + pl.program_id(1) * tc

    def issue(r, carry):
        _row_copy(ys_ref, a_ref, p0_ref[base + r], r, sem.at[0]).start()
        _row_copy(ys_ref, b_ref, p1_ref[base + r], r, sem.at[1]).start()
        return carry

    lax.fori_loop(0, tc, issue, 0, unroll=ROW_DMA_UNROLL)

    def drain(r, carry):
        _row_copy(ys_ref, a_ref, 0, r, sem.at[0]).wait()
        _row_copy(ys_ref, b_ref, 0, r, sem.at[1]).wait()
        return carry

    lax.fori_loop(0, tc, drain, 0, unroll=ROW_DMA_UNROLL)
    r = r_ref[...]
    o_ref[...] = x_ref[...] + gt_ref[...] * (r[:, 2:3] * a_ref[...] + r[:, 3:4] * b_ref[...])


def moe_combine(ys, pos0, pos1, x, gate, route, *, tc):
    gsz, t, d = x.shape
    if gate.shape[1] == 1:
        gspec = pl.BlockSpec((None, 1, d), lambda g_, i, a, b: (g_, 0, 0))
    else:
        gspec = pl.BlockSpec((None, tc, d), lambda g_, i, a, b: (g_, i, 0))
    grid_spec = pltpu.PrefetchScalarGridSpec(
        num_scalar_prefetch=2,
        grid=(gsz, t // tc),
        in_specs=[pl.BlockSpec(memory_space=pl.ANY),
                  pl.BlockSpec((None, tc, d), lambda g_, i, a, b: (g_, i, 0)),
                  gspec,
                  pl.BlockSpec((None, tc, LANE), lambda g_, i, a, b: (g_, i, 0))],
        out_specs=pl.BlockSpec((None, tc, d), lambda g_, i, a, b: (g_, i, 0)),
        scratch_shapes=[pltpu.VMEM((tc, d), F32), pltpu.VMEM((tc, d), F32), pltpu.SemaphoreType.DMA((2,))])
    return pl.pallas_call(
        functools.partial(_combine_kernel, tc=tc, t=t),
        grid_spec=grid_spec,
        out_shape=jax.ShapeDtypeStruct((gsz, t, d), F32),
        compiler_params=_cparams("arbitrary", "arbitrary"),
        name="moe_combine",
    )(pos0, pos1, ys, x, gate, route)


def _routing_tables(idx, tm):
    m = idx.shape[0]
    e_flat = idx.reshape(-1)
    onehot = (e_flat[:, None] == jnp.arange(N_EXP, dtype=I32)[None, :]).astype(I32)
    rank = jnp.take_along_axis(jnp.cumsum(onehot, axis=0) - onehot, e_flat[:, None], axis=1)[:, 0]
    counts = jnp.sum(onehot, axis=0)
    padded = ((counts + tm - 1) // tm) * tm
    starts = jnp.cumsum(padded) - padded
    pos = starts[e_flat] + rank
    n_rows = ((2 * m + N_EXP * (tm - 1)) // tm) * tm
    n_tiles = n_rows // tm
    row_token = jnp.zeros((n_rows,), I32).at[pos].set(jnp.arange(2 * m, dtype=I32) // 2)
    tile_start = jnp.arange(n_tiles, dtype=I32) * tm
    ends = jnp.cumsum(padded)
    tile_e = jnp.sum((tile_start[:, None] >= ends[None, :]).astype(I32), axis=1)
    total = ends[-1]
    active = (tile_start < total).astype(I32)
    last_e = jnp.max(jnp.where(counts > 0, jnp.arange(N_EXP, dtype=I32), 0))
    tile_e = jnp.where(active == 1, jnp.minimum(tile_e, N_EXP - 1), last_e).astype(I32)
    return row_token, tile_e, active, pos.reshape(m, 2)


def moe_residual(x, g, shift, scale, gate, w_router, wg, wu, wd, li, *, tm_tok, tm, tf, precise=False):
    gsz, t, d = x.shape
    h, r = moe_router(x, g, shift, scale, w_router, li, tm=tm_tok)
    idx = r.reshape(gsz * t, LANE)[:, 0:2].astype(I32)
    row_token, tile_e, active, pos = _routing_tables(idx, tm)
    xs = gather_rows(h.reshape(gsz * t, d), row_token, tg=min(tm, 256))
    ys = moe_experts(xs, tile_e, active, wg, wu, wd, li, tm=tm, tf=tf, precise=precise)
    return moe_combine(ys, pos[:, 0], pos[:, 1], x, gate, r, tc=min(tm_tok, 256))


def _pad_ab_weight(w):
    s = [QKV_A, H_A * DV_A, H_A, H_A, H_B * DH_B, HKV_B * DH_B, HKV_B * DH_B, H_IDX * D_IDX, D_IDX, H_IDX]
    o = [0]
    for n in s:
        o.append(o[-1] + n)
    qkv, z, a, b, qb, kb, vb, qi, ki, wi = (w[:, :, o[j]:o[j + 1]] for j in range(10))
    zeros = lambda n: jnp.zeros(w.shape[:2] + (n,), w.dtype)
    return jnp.concatenate([qkv, z, qb, kb, vb, qi, ki, zeros(LANE - D_IDX), a, b, wi, zeros(LANE - 2 * H_A - H_IDX)], axis=-1)


def _tok_tile(t, want):
    return want if t % want == 0 else t


def _trunk(x, mods, w, past, ab_w_pad, tokens_per_seq):
    depth = w['w_ada'].shape[0]
    gsz, t, d = x.shape
    nseq = gsz * t // tokens_per_seq
    ts = tokens_per_seq
    decode = past is not None
    new = {name: [] for name in ('a_s', 'a_conv', 'b_k', 'b_v', 'b_kidx', 'c_conv', 'd_k', 'd_v')}
    tm = _tok_tile(t, 512)

    def seq(a):
        return a.reshape(nseq, ts, a.shape[-1])

    def tok(a):
        return a.reshape(gsz, t, a.shape[-1])

    for layer in range(depth):
        li = layer // 2
        shift1, scale1, gate1, shift2, scale2, gate2 = mods[layer]
        g_mix = w['norm_mix'][layer].reshape(1, d)
        g_ffn = w['norm_ffn'][layer].reshape(1, d)
        if layer % 2 == 0:
            proj = mod_linear(x, g_mix, shift1, scale1, ab_w_pad, li, tm=_tok_tile(t, 1024), tn=AB_COLS // 2,
                              precise=decode)
            qn = head_norm(proj, OFF_QB // 512, 512, w['ab_qnorm'][li], DH_B, tm=tm, precise=decode)
            kn = head_norm(proj, OFF_KV // 128, 128, w['ab_knorm'][li], DH_B, tm=tm, precise=decode)
            projs = seq(proj)
            v_b = projs[:, :, OFF_KV + 128:OFF_KV + 256]
            k_i = projs[:, :, OFF_KI:OFF_KI + D_IDX]
            kns = seq(kn)
            if not decode:
                conv0 = jnp.zeros((nseq, SUBLANE, QKV_A), F32)
                s0 = jnp.zeros((nseq, H_A, DK_A, DV_A), F32)
                o_a, conv_t, s_new = gdn(projs, conv0, s0, w['ab_conv_w'][li], w['ab_conv_b'][li], w['ab_a_log'][li],
                                         w['ab_dt_bias'][li], w['ab_onorm'][li], chunk=CHUNK_A, valid_len=CHUNK_A)
                o_b = dsa_prompt(projs, seq(qn).astype(BF16), jnp.swapaxes(kns, 1, 2).astype(BF16),
                                 v_b.astype(BF16), jnp.swapaxes(k_i, 1, 2).astype(BF16), tq=128,
                                 group=math.gcd(4, ts // 128))
            else:
                cpad = SUBLANE
                conv0 = jnp.pad(past['a_conv'][li], ((0, 0), (SUBLANE - (CONV_A - 1), 0), (0, 0)))
                proj_p = jnp.pad(projs, ((0, 0), (0, cpad - ts), (0, 0)))
                o_a, conv_t, s_new = gdn(proj_p, conv0, past['a_s'][li], w['ab_conv_w'][li], w['ab_conv_b'][li],
                                         w['ab_a_log'][li], w['ab_dt_bias'][li], w['ab_onorm'][li],
                                         chunk=cpad, valid_len=ts, precise=True)
                o_a = o_a[:, :ts]
                o_b = _dsa_decode_wrap(projs, seq(qn), kns, v_b, k_i, past, li, ts)
            new['a_s'].append(s_new)
            new['a_conv'].append(conv_t[:, SUBLANE - (CONV_A - 1):])
            new['b_k'].append(kns.reshape(nseq, ts, HKV_B, DH_B))
            new['b_v'].append(v_b.reshape(nseq, ts, HKV_B, DH_B))
            new['b_kidx'].append(k_i)
            x = out_residual(tok(o_a), tok(o_b), w['ab_w_out'], li, x, gate1, tm=tm, precise=decode)
            x = ffn_residual(x, g_ffn, shift2, scale2, gate2, w['ff_w_gate'], w['ff_w_up'], w['ff_w_down'], li,
                             tm=_tok_tile(t, 1024), tf=256, precise=decode)
        else:
            proj = mod_linear(x, g_mix, shift1, scale1, w['cd_w_in'], li, tm=_tok_tile(t, 1024), tn=1280,
                              precise=decode)
            qn = head_norm(proj, 2, 512, w['cd_qnorm'][li], DH_D, tm=tm, precise=decode)
            kn = head_norm(proj, 3, 512, w['cd_knorm'][li], DH_D, tm=tm, precise=decode)
            projs = seq(proj)
            v_d = projs[:, :, 4 * C_W:5 * C_W]
            kns, qns = seq(kn), seq(qn)
            halo = 4 * SUBLANE
            if not decode:
                conv0 = jnp.zeros((nseq, halo, C_W), F32)
                o_c, conv_t = conformer_conv(projs, conv0, w['cd_conv_w'][li], w['cd_conv_b'][li], w['cd_ln_g'][li],
                                             w['cd_ln_b'][li], tt=512, valid_len=512)
                heads = lambda a: a.reshape(nseq, ts, H_D, DH_D).transpose(0, 2, 1, 3).astype(BF16)
                o_d = sb_prompt(heads(qns), heads(kns), heads(v_d), tq=256)
                o_d = o_d.transpose(0, 2, 1, 3).reshape(nseq, ts, H_D * DH_D)
            else:
                conv0 = jnp.pad(past['c_conv'][li], ((0, 0), (halo - (CONV_C - 1), 0), (0, 0)))
                proj_p = jnp.pad(projs, ((0, 0), (0, SUBLANE - ts), (0, 0)))
                o_c, conv_t = conformer_conv(proj_p, conv0, w['cd_conv_w'][li], w['cd_conv_b'][li], w['cd_ln_g'][li],
                                             w['cd_ln_b'][li], tt=SUBLANE, valid_len=ts)
                o_c = o_c[:, :ts]
                o_d = _sb_decode_wrap(qns, kns, v_d, past, li, ts)
            new['c_conv'].append(conv_t[:, halo - (CONV_C - 1):])
            new['d_k'].append(kns.reshape(nseq, ts, H_D, DH_D))
            new['d_v'].append(v_d.reshape(nseq, ts, H_D, DH_D))
            x = out_residual(tok(o_c), tok(o_d), w['cd_w_out'], li, x, gate1, tm=tm, precise=decode)
            x = moe_residual(x, g_ffn, shift2, scale2, gate2, w['moe_router'], w['moe_w_gate'], w['moe_w_up'],
                             w['moe_w_down'], li, tm_tok=tm, tm=_tok_tile(t, 1024), tf=512, precise=decode)
    stacked = (jnp.stack(new['a_s'], 0), jnp.stack(new['a_conv'], 0),
               jnp.stack(new['b_k'], 2), jnp.stack(new['b_v'], 2), jnp.stack(new['b_kidx'], 2),
               jnp.stack(new['c_conv'], 0), jnp.stack(new['d_k'], 2), jnp.stack(new['d_v'], 2))
    return x, stacked


def _dsa_decode_wrap(projs, qns, kns, v_b, k_i, past, li, ts):
    nseq = projs.shape[0]
    page = past['b_k'].shape[1]
    n_pool, n_layers = past['b_k'].shape[0], past['b_k'].shape[2]
    grp = H_B // HKV_B
    kvw = HKV_B * DH_B
    qi = projs[:, :, OFF_QI:OFF_QI + H_IDX * D_IDX].reshape(nseq, ts, H_IDX, D_IDX).transpose(0, 2, 1, 3)
    qi = jnp.pad(qi, ((0, 0), (0, 0), (0, SUBLANE - ts), (0, 0)))
    w_idx = jnp.pad(projs[:, :, OFF_SMALL + 2 * H_A:OFF_SMALL + 2 * H_A + H_IDX], ((0, 0), (0, SUBLANE - ts), (0, 0)))
    q = qns.reshape(nseq, ts, HKV_B, grp, DH_B).transpose(0, 2, 3, 1, 4)
    q = jnp.pad(q, ((0, 0), (0, 0), (0, 0), (0, SUBLANE - ts), (0, 0)))
    q = q.reshape(nseq, HKV_B, grp * SUBLANE, DH_B)
    q = jnp.stack([jnp.pad(q[:, g], ((0, 0), (0, 0), (g * DH_B, kvw - (g + 1) * DH_B))) for g in range(HKV_B)], 1)
    new_t = lambda a: jnp.swapaxes(jnp.pad(a, ((0, 0), (0, page - ts), (0, 0))), 1, 2)
    pool_k = jnp.transpose(past['b_k'], (0, 2, 3, 4, 1)).reshape(n_pool, n_layers, kvw, page)
    pool_v = jnp.transpose(past['b_v'], (0, 2, 3, 4, 1)).reshape(n_pool, n_layers, kvw, page)
    pool_ki = jnp.transpose(past['b_kidx'], (0, 2, 3, 1))
    o = dsa_decode(past['page_table'], qi, w_idx, q, new_t(kns), new_t(v_b), new_t(k_i), pool_k, pool_v, pool_ki, li,
                   npp=math.gcd(PAGES_PER_STEP, past['page_table'].shape[1]), t_new=ts)
    return o[:, :ts]


def _sb_decode_wrap(qns, kns, v_d, past, li, ts):
    nseq = qns.shape[0]
    page = past['d_k'].shape[1]
    n_layers = past['d_k'].shape[2]
    width = H_D * DH_D
    q = qns.reshape(nseq, ts, H_D, DH_D)
    eye = jnp.eye(H_D, dtype=F32)
    qb = (q[:, :, :, None, :] * eye[None, None, :, :, None]).reshape(nseq, ts * H_D, width)
    new_t = lambda a: jnp.swapaxes(jnp.pad(a, ((0, 0), (0, page - ts), (0, 0))), 1, 2)
    pool_t = lambda a: jnp.transpose(a, (0, 2, 3, 4, 1))
    return sb_decode(past['page_table'], qb, new_t(kns), new_t(v_d), pool_t(past['d_k']), pool_t(past['d_v']), li)


def kernel(x_prompt, x_sample, state_a_s, state_a_conv, cache_b_k, cache_b_v, cache_b_kidx, state_c_conv, cache_d_k, cache_d_v, page_table, c_prompt, c_sample, w_ada, b_ada, norm_mix, norm_ffn, ab_w_in, ab_conv_w, ab_conv_b, ab_a_log, ab_dt_bias, ab_onorm, ab_qnorm, ab_knorm, ab_w_out, cd_w_in, cd_conv_w, cd_conv_b, cd_ln_g, cd_ln_b, cd_qnorm, cd_knorm, cd_w_out, ff_w_gate, ff_w_up, ff_w_down, moe_router, moe_w_gate, moe_w_up, moe_w_down):
    w = {'w_ada': w_ada, 'b_ada': b_ada, 'norm_mix': norm_mix, 'norm_ffn': norm_ffn,
         'ab_conv_w': ab_conv_w, 'ab_conv_b': ab_conv_b, 'ab_a_log': ab_a_log,
         'ab_dt_bias': ab_dt_bias, 'ab_onorm': ab_onorm, 'ab_qnorm': ab_qnorm, 'ab_knorm': ab_knorm,
         'ab_w_out': ab_w_out, 'cd_w_in': cd_w_in, 'cd_conv_w': cd_conv_w, 'cd_conv_b': cd_conv_b,
         'cd_ln_g': cd_ln_g, 'cd_ln_b': cd_ln_b, 'cd_qnorm': cd_qnorm, 'cd_knorm': cd_knorm,
         'cd_w_out': cd_w_out, 'ff_w_gate': ff_w_gate, 'ff_w_up': ff_w_up, 'ff_w_down': ff_w_down,
         'moe_router': moe_router, 'moe_w_gate': moe_w_gate, 'moe_w_up': moe_w_up, 'moe_w_down': moe_w_down}
    past = {'a_s': state_a_s, 'a_conv': state_a_conv, 'b_k': cache_b_k, 'b_v': cache_b_v,
            'b_kidx': cache_b_kidx, 'c_conv': state_c_conv, 'd_k': cache_d_k, 'd_v': cache_d_v,
            'page_table': page_table}
    bp, tp, d = x_prompt.shape
    bs, tsq, _ = x_sample.shape
    depth = w_ada.shape[0]
    rows = bp + bs
    rpad = -rows % SUBLANE
    c_all = jnp.pad(jnp.concatenate([c_prompt, c_sample], axis=0), ((0, rpad), (0, 0)))
    mod = ada_ln(c_all, w_ada, b_ada)
    mod = mod.reshape(depth, rows + rpad, 6, d)
    mods_p = [[mod[l, :bp, j].reshape(bp, 1, d) for j in range(6)] for l in range(depth)]
    mods_s = [[jnp.repeat(mod[l, bp:bp + bs, j], tsq, axis=0).reshape(1, bs * tsq, d) for j in range(6)]
              for l in range(depth)]
    ab_w_pad = _pad_ab_weight(ab_w_in)
    y_p, (pa_s, pa_conv, pb_k, pb_v, pb_kidx, pc_conv, pd_k, pd_v) = _trunk(x_prompt, mods_p, w, None, ab_w_pad, tp)
    y_s, (sa_s, sa_conv, sb_k, sb_v, sb_kidx, sc_conv, sd_k, sd_v) = _trunk(
        x_sample.reshape(1, bs * tsq, d), mods_s, w, past, ab_w_pad, tsq)
    return (y_p, y_s.reshape(bs, tsq, d), pa_s, sa_s, pa_conv, sa_conv, pb_k, sb_k, pb_v, sb_v, pb_kidx, sb_kidx,
            pc_conv, sc_conv, pd_k, sd_k, pd_v, sd_v)
```
